```python
import jax, jax.numpy as jnp
from jax import lax
import numpy as np

D_MODEL = 4096
BATCH = 4
SEQ = 2048
DEPTH = 2
DEC_BATCH = 128
DEC_SEQ = 8
PAST_LEN = 16384
PAGE_SIZE = 128

N_MIXERS = 2
N_SGU_LAYERS = (DEPTH + 1) // 2
N_CONV_LAYERS = DEPTH // 2
D_FF = 11008
CHUNK = 128
SGU_INNER = D_MODEL
SGU_GROUPS = 16
SGU_GROUP_DIM = SGU_INNER // SGU_GROUPS
CONV_WIDTH = 31
PLE_DIM = 256
ALPHA = (2 * DEPTH) ** 0.25
BETA = (8 * DEPTH) ** -0.25
LN_EPS = 1e-5

kernel_name = 'hybrid_sgu_conformer_decoder_step'


def _layer_norm(x, g, b):
    xf = x.astype(jnp.float32)
    mu = jnp.mean(xf, axis=-1, keepdims=True)
    var = jnp.mean(jnp.square(xf - mu), axis=-1, keepdims=True)
    y = (xf - mu) * lax.rsqrt(var + LN_EPS) * g.astype(jnp.float32) + b.astype(jnp.float32)
    return y.astype(x.dtype)


def _post_norm(x, delta, g, b):
    return _layer_norm(ALPHA * x + delta, g, b)


def _swiglu(x, w_gate, w_up, w_down):
    h = jax.nn.silu(jnp.einsum('bld,df->blf', x, w_gate)) * jnp.einsum('bld,df->blf', x, w_up)
    return jnp.einsum('blf,fd->bld', h, w_down)


def _spatial_gating_mixer(x, w_in, ln_g, ln_b, w_s, b_s, w_out):
    bsz, seq_len, _ = x.shape
    z = jax.nn.gelu(jnp.einsum('bld,de->ble', x, w_in), approximate=False)
    u, v = jnp.split(z, 2, axis=-1)
    v = _layer_norm(v, ln_g, ln_b)
    n_chunks = -(-seq_len // CHUNK)
    pad = n_chunks * CHUNK - seq_len
    vc = jnp.pad(v, ((0, 0), (0, pad), (0, 0))).reshape(bsz, n_chunks, CHUNK, SGU_GROUPS, SGU_GROUP_DIM)
    causal = jnp.tril(jnp.ones((CHUNK, CHUNK), dtype=bool))
    w_causal = jnp.where(causal[None], w_s, 0).astype(v.dtype)
    s = jnp.einsum('gts,bcsgd->bctgd', w_causal, vc) + jnp.transpose(b_s)[:, :, None]
    s = s.reshape(bsz, n_chunks * CHUNK, SGU_INNER)[:, :seq_len]
    return jnp.einsum('ble,ed->bld', u * s, w_out), v


def _conv_module(x, buf, w_pw1, w_dw, b_dw, ln_g, ln_b, w_pw2):
    h = jnp.einsum('bld,de->ble', x, w_pw1)
    a, g = jnp.split(h, 2, axis=-1)
    glu = a * jax.nn.sigmoid(g)
    xpad = jnp.concatenate([buf.astype(glu.dtype), glu], axis=1)
    y = lax.conv_general_dilated(
        xpad, w_dw[:, None, :].astype(glu.dtype), window_strides=(1,), padding='VALID',
        dimension_numbers=('NWC', 'WIO', 'NWC'), feature_group_count=D_MODEL) + b_dw
    y = jax.nn.silu(_layer_norm(y, ln_g, ln_b))
    return jnp.einsum('bld,de->ble', y, w_pw2), xpad[:, -(CONV_WIDTH - 1):]


def _ple(x, p, w_gate, w_proj):
    gate = jax.nn.sigmoid(jnp.einsum('bld,de->ble', x, w_gate))
    return gate * jnp.einsum('blp,pd->bld', p, w_proj)


def setup_inputs(seed: int = 0) -> dict:
    key = jax.random.key(seed)
    ks = jax.random.split(key, 32)

    def nrm(k, shape, scale):
        return jax.random.normal(k, shape, jnp.float32) * scale

    return {
        'x_prompt': nrm(ks[0], (BATCH, SEQ, D_MODEL), 1.0),
        'x_sample': nrm(ks[1], (DEC_BATCH, DEC_SEQ, D_MODEL), 1.0),
        'state_conv': nrm(ks[2], (N_CONV_LAYERS, DEC_BATCH, CONV_WIDTH - 1, D_MODEL), 0.5),
        'p_prompt': nrm(ks[3], (DEPTH, BATCH, SEQ, PLE_DIM), 1.0),
        'p_sample': nrm(ks[4], (DEPTH, DEC_BATCH, DEC_SEQ, PLE_DIM), 1.0),
        'ln_g': 1.0 + nrm(ks[5], (DEPTH, 4, D_MODEL), 0.01),
        'ln_b': nrm(ks[6], (DEPTH, 4, D_MODEL), 0.01),
        'ffn_w_gate': nrm(ks[7], (DEPTH, 2, D_MODEL, D_FF), D_MODEL ** -0.5),
        'ffn_w_up': nrm(ks[8], (DEPTH, 2, D_MODEL, D_FF), D_MODEL ** -0.5),
        'ffn_w_down': nrm(ks[9], (DEPTH, 2, D_FF, D_MODEL), BETA * D_FF ** -0.5),
        'sgu_w_in': nrm(ks[10], (N_SGU_LAYERS, D_MODEL, 2 * SGU_INNER), D_MODEL ** -0.5),
        'sgu_ln_g': 1.0 + nrm(ks[11], (N_SGU_LAYERS, SGU_INNER), 0.01),
        'sgu_ln_b': nrm(ks[12], (N_SGU_LAYERS, SGU_INNER), 0.01),
        'sgu_w_s': nrm(ks[13], (N_SGU_LAYERS, SGU_GROUPS, CHUNK, CHUNK), CHUNK ** -0.5),
        'sgu_b_s': 1.0 + nrm(ks[14], (N_SGU_LAYERS, SGU_GROUPS, CHUNK), 0.01),
        'sgu_w_out': nrm(ks[15], (N_SGU_LAYERS, SGU_INNER, D_MODEL), BETA * SGU_INNER ** -0.5),
        'conv_w_pw1': nrm(ks[16], (N_CONV_LAYERS, D_MODEL, 2 * D_MODEL), D_MODEL ** -0.5),
        'conv_w_dw': nrm(ks[17], (N_CONV_LAYERS, CONV_WIDTH, D_MODEL), CONV_WIDTH ** -0.5),
        'conv_b_dw': nrm(ks[18], (N_CONV_LAYERS, D_MODEL), 0.01),
        'conv_ln_g': 1.0 + nrm(ks[19], (N_CONV_LAYERS, D_MODEL), 0.01),
        'conv_ln_b': nrm(ks[20], (N_CONV_LAYERS, D_MODEL), 0.01),
        'conv_w_pw2': nrm(ks[21], (N_CONV_LAYERS, D_MODEL, D_MODEL), BETA * D_MODEL ** -0.5),
        'ple_w_gate': nrm(ks[22], (DEPTH, D_MODEL, D_MODEL), D_MODEL ** -0.5),
        'ple_w_proj': nrm(ks[23], (DEPTH, PLE_DIM, D_MODEL), BETA * PLE_DIM ** -0.5),
    }


def reference(x_prompt, x_sample, state_conv, p_prompt, p_sample, ln_g, ln_b,
              ffn_w_gate, ffn_w_up, ffn_w_down,
              sgu_w_in, sgu_ln_g, sgu_ln_b, sgu_w_s, sgu_b_s, sgu_w_out,
              conv_w_pw1, conv_w_dw, conv_b_dw, conv_ln_g, conv_ln_b, conv_w_pw2,
              ple_w_gate, ple_w_proj):
    y_p, y_s = x_prompt, x_sample
    new_v_s, new_conv_p, new_conv_s = [], [], []
    for i in range(DEPTH):
        y_p = _post_norm(y_p, 0.5 * _swiglu(y_p, ffn_w_gate[i, 0], ffn_w_up[i, 0], ffn_w_down[i, 0]), ln_g[i, 0], ln_b[i, 0])
        y_s = _post_norm(y_s, 0.5 * _swiglu(y_s, ffn_w_gate[i, 0], ffn_w_up[i, 0], ffn_w_down[i, 0]), ln_g[i, 0], ln_b[i, 0])
        j = i // N_MIXERS
        if i % N_MIXERS == 0:
            m_p, _ = _spatial_gating_mixer(y_p, sgu_w_in[j], sgu_ln_g[j], sgu_ln_b[j], sgu_w_s[j], sgu_b_s[j], sgu_w_out[j])
            m_s, v_s = _spatial_gating_mixer(y_s, sgu_w_in[j], sgu_ln_g[j], sgu_ln_b[j], sgu_w_s[j], sgu_b_s[j], sgu_w_out[j])
            new_v_s.append(v_s)
        else:
            zero_buf = jnp.zeros((y_p.shape[0], CONV_WIDTH - 1, D_MODEL), y_p.dtype)
            m_p, c_p = _conv_module(y_p, zero_buf, conv_w_pw1[j], conv_w_dw[j], conv_b_dw[j], conv_ln_g[j], conv_ln_b[j], conv_w_pw2[j])
            m_s, c_s = _conv_module(y_s, state_conv[j], conv_w_pw1[j], conv_w_dw[j], conv_b_dw[j], conv_ln_g[j], conv_ln_b[j], conv_w_pw2[j])
            new_conv_p.append(c_p)
            new_conv_s.append(c_s)
        y_p = _post_norm(y_p, m_p, ln_g[i, 1], ln_b[i, 1])
        y_s = _post_norm(y_s, m_s, ln_g[i, 1], ln_b[i, 1])
        y_p = _post_norm(y_p, 0.5 * _swiglu(y_p, ffn_w_gate[i, 1], ffn_w_up[i, 1], ffn_w_down[i, 1]), ln_g[i, 2], ln_b[i, 2])
        y_s = _post_norm(y_s, 0.5 * _swiglu(y_s, ffn_w_gate[i, 1], ffn_w_up[i, 1], ffn_w_down[i, 1]), ln_g[i, 2], ln_b[i, 2])
        y_p = _post_norm(y_p, _ple(y_p, p_prompt[i], ple_w_gate[i], ple_w_proj[i]), ln_g[i, 3], ln_b[i, 3])
        y_s = _post_norm(y_s, _ple(y_s, p_sample[i], ple_w_gate[i], ple_w_proj[i]), ln_g[i, 3], ln_b[i, 3])
    return (y_p, y_s, jnp.stack(new_v_s), jnp.stack(new_conv_p), jnp.stack(new_conv_s))
```

```python
import functools
import math

import jax
import jax.numpy as jnp
from jax import lax
from jax.experimental import pallas as pl
from jax.experimental.pallas import tpu as pltpu

DEPTH = 2
N_MIXERS = 2
CHUNK = 128
SGU_GROUPS = 16
CONV_WIDTH = 31
ALPHA = (2 * DEPTH) ** 0.25
LN_EPS = 1e-5

SUBLANES = 8
LANES = 128
BF16_ROWS = 16
HALO = 32
VMEM_LIMIT_BYTES = 56 * 1024 * 1024

BF16 = jnp.bfloat16
F32 = jnp.float32


def _params(semantics):
    return pltpu.CompilerParams(dimension_semantics=semantics,
                                vmem_limit_bytes=VMEM_LIMIT_BYTES)


def _layer_norm_rows(z, g, b):
    mu = jnp.mean(z, axis=-1, keepdims=True)
    zc = z - mu
    var = jnp.mean(zc * zc, axis=-1, keepdims=True)
    return zc * lax.rsqrt(var + LN_EPS) * g + b


def _rowwise(n_rows, fn, group=4, slice_rows=SUBLANES):
    step = slice_rows * group
    assert n_rows % step == 0

    def body(c, carry):
        for s in range(group):
            fn(pl.ds(pl.multiple_of(c * step + s * slice_rows, slice_rows), slice_rows))
        return carry

    lax.fori_loop(0, n_rows // step, body, 0)


def _residual_ln(x_ref, d_ref, g_ref, b_ref, o_ref, delta_scale):
    g = g_ref[...]
    b = b_ref[...]

    def one(rows):
        d = d_ref[rows, :]
        if delta_scale != 1.0:
            d = delta_scale * d
        o_ref[rows, :] = _layer_norm_rows(ALPHA * x_ref[rows, :] + d, g, b)

    _rowwise(o_ref.shape[0], one)


def _ffn_kernel(x_ref, wg_ref, wu_ref, wd_ref, g_ref, b_ref, o_ref, xb_ref, *, nf):
    f = pl.program_id(1)

    @pl.when(f == 0)
    def _():
        xb_ref[...] = x_ref[...].astype(BF16)
        o_ref[...] = jnp.zeros_like(o_ref)

    xb = xb_ref[...]
    gate = jnp.dot(xb, wg_ref[...], preferred_element_type=F32)
    up = jnp.dot(xb, wu_ref[...], preferred_element_type=F32)
    h = (gate * jax.nn.sigmoid(gate) * up).astype(BF16)
    o_ref[...] += jnp.dot(h, wd_ref[...], preferred_element_type=F32)

    @pl.when(f == nf - 1)
    def _():
        _residual_ln(x_ref, o_ref, g_ref, b_ref, o_ref, 0.5)


def _ffn(x, wg, wu, wd, layer, which, ln_g, ln_b, *, tm=512, tf=256):
    m, d = x.shape
    dff = wg.shape[-1]
    nf = dff // tf
    assert m % tm == 0 and dff % tf == 0
    return pl.pallas_call(
        functools.partial(_ffn_kernel, nf=nf),
        grid=(m // tm, nf),
        in_specs=[
            pl.BlockSpec((tm, d), lambda i, f: (i, 0)),
            pl.BlockSpec((None, None, d, tf), lambda i, f: (layer, which, 0, f)),
            pl.BlockSpec((None, None, d, tf), lambda i, f: (layer, which, 0, f)),
            pl.BlockSpec((None, None, tf, d), lambda i, f: (layer, which, f, 0)),
            pl.BlockSpec((1, d), lambda i, f: (0, 0)),
            pl.BlockSpec((1, d), lambda i, f: (0, 0)),
        ],
        out_specs=pl.BlockSpec((tm, d), lambda i, f: (i, 0)),
        out_shape=jax.ShapeDtypeStruct((m, d), F32),
        scratch_shapes=[pltpu.VMEM((tm, d), BF16)],
        compiler_params=_params(("parallel", "arbitrary")),
        name="ffn",
    )(x, wg, wu, wd, ln_g, ln_b)


def _proj_ln_kernel(a_ref, w_ref, x_ref, g_ref, b_ref, o_ref, *, nn, tn):
    n = pl.program_id(1)
    acc = jnp.dot(a_ref[...], w_ref[...], preferred_element_type=F32)
    o_ref[:, pl.ds(pl.multiple_of(n * tn, tn), tn)] = acc

    @pl.when(n == nn - 1)
    def _():
        _residual_ln(x_ref, o_ref, g_ref, b_ref, o_ref, 1.0)


def _proj_ln(a, w, x, ln_g, ln_b, *, tm=256, tn=512):
    m, k = a.shape
    d = w.shape[-1]
    nn = d // tn
    assert m % tm == 0 and d % tn == 0
    return pl.pallas_call(
        functools.partial(_proj_ln_kernel, nn=nn, tn=tn),
        grid=(m // tm, nn),
        in_specs=[
            pl.BlockSpec((tm, k), lambda i, n: (i, 0)),
            pl.BlockSpec((k, tn), lambda i, n: (0, n)),
            pl.BlockSpec((tm, d), lambda i, n: (i, 0)),
            pl.BlockSpec((1, d), lambda i, n: (0, 0)),
            pl.BlockSpec((1, d), lambda i, n: (0, 0)),
        ],
        out_specs=pl.BlockSpec((tm, d), lambda i, n: (i, 0)),
        out_shape=jax.ShapeDtypeStruct((m, d), F32),
        compiler_params=_params(("parallel", "arbitrary")),
        name="proj_ln",
    )(a, w, x, ln_g, ln_b)


def _ple_kernel(x_ref, wg_ref, p_ref, wp_ref, g_ref, b_ref, o_ref, xb_ref, *, nn, tn):
    n = pl.program_id(1)

    @pl.when(n == 0)
    def _():
        xb_ref[...] = x_ref[...].astype(BF16)

    gate = jax.nn.sigmoid(jnp.dot(xb_ref[...], wg_ref[...], preferred_element_type=F32))
    proj = jnp.dot(p_ref[...].astype(BF16), wp_ref[...], preferred_element_type=F32)
    o_ref[:, pl.ds(pl.multiple_of(n * tn, tn), tn)] = gate * proj

    @pl.when(n == nn - 1)
    def _():
        _residual_ln(x_ref, o_ref, g_ref, b_ref, o_ref, 1.0)


def _ple(x, wg, p, wp, ln_g, ln_b, *, tm=256, tn=512):
    m, d = x.shape
    pdim = p.shape[-1]
    nn = d // tn
    assert m % tm == 0 and d % tn == 0
    return pl.pallas_call(
        functools.partial(_ple_kernel, nn=nn, tn=tn),
        grid=(m // tm, nn),
        in_specs=[
            pl.BlockSpec((tm, d), lambda i, n: (i, 0)),
            pl.BlockSpec((d, tn), lambda i, n: (0, n)),
            pl.BlockSpec((tm, pdim), lambda i, n: (i, 0)),
            pl.BlockSpec((pdim, tn), lambda i, n: (0, n)),
            pl.BlockSpec((1, d), lambda i, n: (0, 0)),
            pl.BlockSpec((1, d), lambda i, n: (0, 0)),
        ],
        out_specs=pl.BlockSpec((tm, d), lambda i, n: (i, 0)),
        out_shape=jax.ShapeDtypeStruct((m, d), F32),
        scratch_shapes=[pltpu.VMEM((tm, d), BF16)],
        compiler_params=_params(("parallel", "arbitrary")),
        name="ple",
    )(x, wg, p, wp, ln_g, ln_b)


def _gelu_exact(x):
    return 0.5 * x * (1.0 + lax.erf(x * math.sqrt(0.5)))


def _sgu_in_kernel(x_ref, w_ref, lg_ref, lb_ref, ws_ref, bt_ref, v_ref, gated_ref,
                   xb_ref, u_ref, *, nn, tn, period):
    n = pl.program_id(1)
    half = nn // 2

    @pl.when(n == 0)
    def _():
        xb_ref[...] = x_ref[...].astype(BF16)

    z = _gelu_exact(jnp.dot(xb_ref[...], w_ref[...], preferred_element_type=F32))

    @pl.when(n < half)
    def _():
        u_ref[:, pl.ds(pl.multiple_of(n * tn, tn), tn)] = z

    @pl.when(n >= half)
    def _():
        v_ref[:, pl.ds(pl.multiple_of((n - half) * tn, tn), tn)] = z

    @pl.when(n == nn - 1)
    def _():
        lg = lg_ref[...]
        lb = lb_ref[...]

        def norm(rows):
            v_ref[rows, :] = _layer_norm_rows(v_ref[rows, :], lg, lb)

        _rowwise(v_ref.shape[0], norm)

        t = lax.broadcasted_iota(jnp.int32, (CHUNK, CHUNK), 0)
        s = lax.broadcasted_iota(jnp.int32, (CHUNK, CHUNK), 1)
        keep = (s <= t) & ((t // period) == (s // period))
        gdim = v_ref.shape[1] // SGU_GROUPS
        for c in range(v_ref.shape[0] // CHUNK):
            rows = slice(c * CHUNK, (c + 1) * CHUNK)
            for g in range(SGU_GROUPS):
                cols = slice(g * gdim, (g + 1) * gdim)
                wm = jnp.where(keep, ws_ref[g], 0.0).astype(BF16)
                sg = jnp.dot(wm, v_ref[rows, cols].astype(BF16), preferred_element_type=F32)
                sg = sg + bt_ref[:, g:g + 1]
                gated_ref[rows, cols] = (u_ref[rows, cols] * sg).astype(BF16)


def _sgu_in(x, w_in, lg, lb, ws, bt, period, *, tm=256, tn=512):
    m, d = x.shape
    inner = w_in.shape[-1] // 2
    nn = w_in.shape[-1] // tn
    assert m % tm == 0 and tm % CHUNK == 0 and inner % tn == 0
    return pl.pallas_call(
        functools.partial(_sgu_in_kernel, nn=nn, tn=tn, period=period),
        grid=(m // tm, nn),
        in_specs=[
            pl.BlockSpec((tm, d), lambda i, n: (i, 0)),
            pl.BlockSpec((d, tn), lambda i, n: (0, n)),
            pl.BlockSpec((1, inner), lambda i, n: (0, 0)),
            pl.BlockSpec((1, inner), lambda i, n: (0, 0)),
            pl.BlockSpec((SGU_GROUPS, CHUNK, CHUNK), lambda i, n: (0, 0, 0)),
            pl.BlockSpec((CHUNK, SGU_GROUPS), lambda i, n: (0, 0)),
        ],
        out_specs=[
            pl.BlockSpec((tm, inner), lambda i, n: (i, 0)),
            pl.BlockSpec((tm, inner), lambda i, n: (i, 0)),
        ],
        out_shape=[
            jax.ShapeDtypeStruct((m, inner), F32),
            jax.ShapeDtypeStruct((m, inner), BF16),
        ],
        scratch_shapes=[pltpu.VMEM((tm, d), BF16), pltpu.VMEM((tm, inner), F32)],
        compiler_params=_params(("parallel", "arbitrary")),
        name="sgu_in",
    )(x, w_in, lg, lb, ws, bt)


def _glu_kernel(x_ref, wa_ref, wg_ref, o_ref, xb_ref):
    @pl.when(pl.program_id(1) == 0)
    def _():
        xb_ref[...] = x_ref[...].astype(BF16)

    xb = xb_ref[...]
    a = jnp.dot(xb, wa_ref[...], preferred_element_type=F32)
    g = jnp.dot(xb, wg_ref[...], preferred_element_type=F32)
    o_ref[...] = a * jax.nn.sigmoid(g)


def _glu(x, w_pw1, *, tm=512, tn=512):
    m, d = x.shape
    inner = w_pw1.shape[-1] // 2
    nn = inner // tn
    assert m % tm == 0 and inner % tn == 0
    return pl.pallas_call(
        _glu_kernel,
        grid=(m // tm, nn),
        in_specs=[
            pl.BlockSpec((tm, d), lambda i, n: (i, 0)),
            pl.BlockSpec((d, tn), lambda i, n: (0, n)),
            pl.BlockSpec((d, tn), lambda i, n: (0, n + nn)),
        ],
        out_specs=pl.BlockSpec((tm, tn), lambda i, n: (i, n)),
        out_shape=jax.ShapeDtypeStruct((m, inner), F32),
        scratch_shapes=[pltpu.VMEM((tm, d), BF16)],
        compiler_params=_params(("parallel", "arbitrary")),
        name="glu",
    )(x, w_pw1, w_pw1)


CONV_LANES = 256


def _dwconv_block(blk, j0, n_out, w_ref, cols):
    acc = None
    for r in range(SUBLANES):
        taps = [k for k in range(CONV_WIDTH) if (j0 + k) % SUBLANES == r]
        if not taps:
            continue
        span = max(j0 + k - r for k in taps) + n_out
        shifted = blk[r:r + span]
        for k in taps:
            a = j0 + k - r
            term = w_ref[k:k + 1, cols] * shifted[a:a + n_out]
            acc = term if acc is None else acc + term
    return acc


def _conv_post(y_ref, bd_ref, g_ref, b_ref, o_ref, group):
    bd = bd_ref[...]
    g = g_ref[...]
    b = b_ref[...]

    def one(rows):
        z = _layer_norm_rows(y_ref[rows, :] + bd, g, b)
        o_ref[rows, :] = (z * jax.nn.sigmoid(z)).astype(o_ref.dtype)

    _rowwise(y_ref.shape[0], one, group, BF16_ROWS)


def _conv_prompt_kernel(cur_ref, prev_ref, w_ref, bd_ref, g_ref, b_ref, o_ref, win_ref, y_ref, *, t_rows):
    t = pl.program_id(1)

    @pl.when(t == 0)
    def _():
        win_ref[0:HALO, :] = jnp.zeros((HALO, win_ref.shape[1]), F32)

    @pl.when(t > 0)
    def _():
        win_ref[0:HALO, :] = prev_ref[...]

    win_ref[HALO:HALO + t_rows, :] = cur_ref[...]

    d = y_ref.shape[-1]
    base = HALO - (CONV_WIDTH - 1)
    n_out = HALO

    def body(rb, carry):
        r0 = pl.multiple_of(rb * n_out, n_out)
        for c0 in range(0, d, CONV_LANES):
            cols = slice(c0, c0 + CONV_LANES)
            blk = win_ref[pl.ds(r0, HALO + n_out), cols]
            y_ref[pl.ds(r0, n_out), cols] = _dwconv_block(blk, base, n_out, w_ref, cols)
        return carry

    lax.fori_loop(0, t_rows // n_out, body, 0)
    _conv_post(y_ref, bd_ref, g_ref, b_ref, o_ref, 2)


def _conv_prompt(glu, w_dw, b_dw, ln_g, ln_b, *, t_rows=256):
    bsz, seq, d = glu.shape
    per = t_rows // HALO
    assert seq % t_rows == 0 and t_rows % HALO == 0
    return pl.pallas_call(
        functools.partial(_conv_prompt_kernel, t_rows=t_rows),
        grid=(bsz, seq // t_rows),
        in_specs=[
            pl.BlockSpec((None, t_rows, d), lambda b, t: (b, t, 0)),
            pl.BlockSpec((None, HALO, d), lambda b, t: (b, jnp.maximum(t * per - 1, 0), 0)),
            pl.BlockSpec((CONV_WIDTH, d), lambda b, t: (0, 0)),
            pl.BlockSpec((1, d), lambda b, t: (0, 0)),
            pl.BlockSpec((1, d), lambda b, t: (0, 0)),
            pl.BlockSpec((1, d), lambda b, t: (0, 0)),
        ],
        out_specs=pl.BlockSpec((None, t_rows, d), lambda b, t: (b, t, 0)),
        out_shape=jax.ShapeDtypeStruct((bsz, seq, d), BF16),
        scratch_shapes=[pltpu.VMEM((HALO + t_rows, d), F32), pltpu.VMEM((t_rows, d), F32)],
        compiler_params=_params(("parallel", "arbitrary")),
        name="conv_prompt",
    )(glu, glu, w_dw, b_dw, ln_g, ln_b)


def _conv_sample_kernel(xpad_ref, w_ref, bd_ref, g_ref, b_ref, o_ref, y_ref, *, bb, seq):
    d = y_ref.shape[-1]
    for j in range(bb):
        for c0 in range(0, d, CONV_LANES):
            cols = slice(c0, c0 + CONV_LANES)
            y_ref[j * seq:(j + 1) * seq, cols] = _dwconv_block(xpad_ref[j, :, cols], 0, seq, w_ref, cols)
    _conv_post(y_ref, bd_ref, g_ref, b_ref, o_ref, 1)


def _conv_sample(xpad, w_dw, b_dw, ln_g, ln_b, *, bb=2):
    bsz, padded, d = xpad.shape
    seq = padded - (CONV_WIDTH - 1)
    assert bsz % bb == 0 and seq % SUBLANES == 0 and (bb * seq) % BF16_ROWS == 0
    return pl.pallas_call(
        functools.partial(_conv_sample_kernel, bb=bb, seq=seq),
        grid=(bsz // bb,),
        in_specs=[
            pl.BlockSpec((bb, padded, d), lambda b: (b, 0, 0)),
            pl.BlockSpec((CONV_WIDTH, d), lambda b: (0, 0)),
            pl.BlockSpec((1, d), lambda b: (0, 0)),
            pl.BlockSpec((1, d), lambda b: (0, 0)),
            pl.BlockSpec((1, d), lambda b: (0, 0)),
        ],
        out_specs=pl.BlockSpec((bb * seq, d), lambda b: (b, 0)),
        out_shape=jax.ShapeDtypeStruct((bsz * seq, d), BF16),
        scratch_shapes=[pltpu.VMEM((bb * seq, d), F32)],
        compiler_params=_params(("parallel",)),
        name="conv_sample",
    )(xpad, w_dw, b_dw, ln_g, ln_b)


def kernel(x_prompt, x_sample, state_conv, p_prompt, p_sample, ln_g, ln_b, ffn_w_gate, ffn_w_up, ffn_w_down, sgu_w_in, sgu_ln_g, sgu_ln_b, sgu_w_s, sgu_b_s, sgu_w_out, conv_w_pw1, conv_w_dw, conv_b_dw, conv_ln_g, conv_ln_b, conv_w_pw2, ple_w_gate, ple_w_proj):
    bsz, seq, d = x_prompt.shape
    dbsz, dseq, _ = x_sample.shape
    assert CHUNK % dseq == 0 and seq % CHUNK == 0

    wg = ffn_w_gate.astype(BF16)
    wu = ffn_w_up.astype(BF16)
    wd = ffn_w_down.astype(BF16)
    w_in = sgu_w_in.astype(BF16)
    w_out = sgu_w_out.astype(BF16)
    w_pw1 = conv_w_pw1.astype(BF16)
    w_pw2 = conv_w_pw2.astype(BF16)
    w_pg = ple_w_gate.astype(BF16)
    w_pp = ple_w_proj.astype(BF16)

    def row(v):
        return v.reshape(1, -1)

    y_p = x_prompt.reshape(bsz * seq, d)
    y_s = x_sample.reshape(dbsz * dseq, d)
    new_v_s, new_conv_p, new_conv_s = [], [], []

    for i in range(DEPTH):
        def ffn(y, which, slot):
            return _ffn(y, wg, wu, wd, i, which, row(ln_g[i, slot]), row(ln_b[i, slot]))

        y_p = ffn(y_p, 0, 0)
        y_s = ffn(y_s, 0, 0)

        j = i // N_MIXERS
        g1, b1 = row(ln_g[i, 1]), row(ln_b[i, 1])
        if i % N_MIXERS == 0:
            lg, lb = row(sgu_ln_g[j]), row(sgu_ln_b[j])
            reps = CHUNK // dseq
            ws_p = sgu_w_s[j]
            ws_s = jnp.tile(sgu_w_s[j][:, :dseq, :dseq], (1, reps, reps))
            bt_p = sgu_b_s[j].T
            bt_s = jnp.tile(sgu_b_s[j][:, :dseq], (1, reps)).T
            _, gated_p = _sgu_in(y_p, w_in[j], lg, lb, ws_p, bt_p, CHUNK)
            v_s, gated_s = _sgu_in(y_s, w_in[j], lg, lb, ws_s, bt_s, dseq)
            new_v_s.append(v_s.reshape(dbsz, dseq, -1))
            y_p = _proj_ln(gated_p, w_out[j], y_p, g1, b1)
            y_s = _proj_ln(gated_s, w_out[j], y_s, g1, b1)
        else:
            cg, cb = row(conv_ln_g[j]), row(conv_ln_b[j])
            bd = row(conv_b_dw[j])
            glu_p = _glu(y_p, w_pw1[j]).reshape(bsz, seq, d)
            glu_s = _glu(y_s, w_pw1[j]).reshape(dbsz, dseq, d)
            xpad_s = jnp.concatenate([state_conv[j], glu_s], axis=1)
            c_p = _conv_prompt(glu_p, conv_w_dw[j], bd, cg, cb).reshape(bsz * seq, d)
            c_s = _conv_sample(xpad_s, conv_w_dw[j], bd, cg, cb)
            new_conv_p.append(glu_p[:, seq - (CONV_WIDTH - 1):])
            new_conv_s.append(xpad_s[:, dseq:])
            y_p = _proj_ln(c_p, w_pw2[j], y_p, g1, b1)
            y_s = _proj_ln(c_s, w_pw2[j], y_s, g1, b1)

        y_p = ffn(y_p, 1, 2)
        y_s = ffn(y_s, 1, 2)

        g3, b3 = row(ln_g[i, 3]), row(ln_b[i, 3])
        y_p = _ple(y_p, w_pg[i], p_prompt[i].reshape(bsz * seq, -1), w_pp[i], g3, b3)
        y_s = _ple(y_s, w_pg[i], p_sample[i].reshape(dbsz * dseq, -1), w_pp[i], g3, b3)

    return (y_p.reshape(bsz, seq, d), y_s.reshape(dbsz, dseq, d), jnp.stack(new_v_s),
            jnp.stack(new_conv_p), jnp.stack(new_conv_s))
```

```python
import functools
import math

import jax
import jax.numpy as jnp
from jax import lax
from jax.experimental import pallas as pl
from jax.experimental.pallas import tpu as pltpu

DEPTH = 2
N_MIXERS = 2
CHUNK = 128
SGU_GROUPS = 16
CONV_WIDTH = 31
ALPHA = (2 * DEPTH) ** 0.25
LN_EPS = 1e-5

SUBLANES = 8
LANES = 128
BF16_ROWS = 16
HALO = 32
CONV_LANES = 256
FFN_TF = 256
VMEM_LIMIT_BYTES = 56 * 1024 * 1024

BF16 = jnp.bfloat16
F32 = jnp.float32


def _params(semantics):
    return pltpu.CompilerParams(dimension_semantics=semantics,
                                vmem_limit_bytes=VMEM_LIMIT_BYTES)


def _rowwise(n_rows, fn, group, slice_rows=SUBLANES):
    step = slice_rows * group
    assert n_rows % step == 0

    def body(c, carry):
        for s in range(group):
            fn(pl.ds(pl.multiple_of(c * step + s * slice_rows, slice_rows), slice_rows))
        return carry

    lax.fori_loop(0, n_rows // step, body, 0)


def _stats_scratch(n_rows):
    return pltpu.VMEM((2, n_rows, LANES), F32)


def _layer_norm_sweeps(n_rows, d, z_block, g_ref, b_ref, store_block, stats_ref,
                       slice_rows=SUBLANES, post=None):
    n_col = d // LANES
    inv_d = 1.0 / d

    def stats(rows):
        s1 = s2 = None
        for c in range(n_col):
            z = z_block(rows, slice(c * LANES, (c + 1) * LANES))
            s1 = z if s1 is None else s1 + z
            s2 = z * z if s2 is None else s2 + z * z
        mean = jnp.sum(s1, axis=-1, keepdims=True) * inv_d
        var = jnp.maximum(jnp.sum(s2, axis=-1, keepdims=True) * inv_d - mean * mean, 0.0)
        stats_ref[0, rows, :] = jnp.broadcast_to(mean, (slice_rows, LANES))
        stats_ref[1, rows, :] = jnp.broadcast_to(lax.rsqrt(var + LN_EPS), (slice_rows, LANES))

    def norm(rows):
        mean = stats_ref[0, rows, :]
        rstd = stats_ref[1, rows, :]
        for c in range(n_col):
            cols = slice(c * LANES, (c + 1) * LANES)
            y = (z_block(rows, cols) - mean) * rstd * g_ref[:, cols] + b_ref[:, cols]
            store_block(rows, cols, y if post is None else post(y))

    n_slices = n_rows // slice_rows
    _rowwise(n_rows, stats, math.gcd(n_slices, 8), slice_rows)
    _rowwise(n_rows, norm, math.gcd(n_slices, 4), slice_rows)


def _residual_ln(x_ref, d_ref, g_ref, b_ref, o_ref, stats_ref, delta_scale):
    def z_block(rows, cols):
        dlt = d_ref[rows, cols]
        if delta_scale != 1.0:
            dlt = delta_scale * dlt
        return ALPHA * x_ref[rows, cols] + dlt

    def store(rows, cols, y):
        o_ref[rows, cols] = y

    _layer_norm_sweeps(o_ref.shape[0], o_ref.shape[1], z_block, g_ref, b_ref, store, stats_ref)


def _ffn_kernel(x_ref, wg_ref, wu_ref, wd_ref, g_ref, b_ref, o_ref, xb_ref, stats_ref, *, nf):
    f = pl.program_id(1)

    @pl.when(f == 0)
    def _():
        xb_ref[...] = x_ref[...].astype(BF16)
        o_ref[...] = jnp.zeros_like(o_ref)

    xb = xb_ref[...]
    gate = jnp.dot(xb, wg_ref[...], preferred_element_type=F32)
    up = jnp.dot(xb, wu_ref[...], preferred_element_type=F32)
    h = (gate * jax.nn.sigmoid(gate) * up).astype(BF16)
    o_ref[...] += jnp.dot(h, wd_ref[...], preferred_element_type=F32)

    @pl.when(f == nf - 1)
    def _():
        _residual_ln(x_ref, o_ref, g_ref, b_ref, o_ref, stats_ref, 0.5)


def _ffn(x, wg, wu, wd, layer, which, ln_g, ln_b, *, tm=512, tf=FFN_TF):
    m, d = x.shape
    dff = wg.shape[-1]
    nf = dff // tf
    assert m % tm == 0 and dff % tf == 0
    vec = pl.BlockSpec((1, d), lambda i, f: (0, 0))
    return pl.pallas_call(
        functools.partial(_ffn_kernel, nf=nf),
        grid=(m // tm, nf),
        in_specs=[
            pl.BlockSpec((tm, d), lambda i, f: (i, 0)),
            pl.BlockSpec((None, None, d, tf), lambda i, f: (layer, which, 0, f)),
            pl.BlockSpec((None, None, d, tf), lambda i, f: (layer, which, 0, f)),
            pl.BlockSpec((None, None, tf, d), lambda i, f: (layer, which, f, 0)),
            vec, vec,
        ],
        out_specs=pl.BlockSpec((tm, d), lambda i, f: (i, 0)),
        out_shape=jax.ShapeDtypeStruct((m, d), F32),
        scratch_shapes=[pltpu.VMEM((tm, d), BF16), _stats_scratch(tm)],
        compiler_params=_params(("parallel", "arbitrary")),
        name="ffn",
    )(x, wg, wu, wd, ln_g, ln_b)


def _proj_ln_kernel(a_ref, w_ref, x_ref, g_ref, b_ref, o_ref, stats_ref, *, nn, tn):
    n = pl.program_id(1)
    acc = jnp.dot(a_ref[...], w_ref[...], preferred_element_type=F32)
    o_ref[:, pl.ds(pl.multiple_of(n * tn, tn), tn)] = acc

    @pl.when(n == nn - 1)
    def _():
        _residual_ln(x_ref, o_ref, g_ref, b_ref, o_ref, stats_ref, 1.0)


def _proj_ln(a, w, x, ln_g, ln_b, *, tm=512, tn=512):
    m, k = a.shape
    d = w.shape[-1]
    nn = d // tn
    assert m % tm == 0 and d % tn == 0
    vec = pl.BlockSpec((1, d), lambda i, n: (0, 0))
    return pl.pallas_call(
        functools.partial(_proj_ln_kernel, nn=nn, tn=tn),
        grid=(m // tm, nn),
        in_specs=[
            pl.BlockSpec((tm, k), lambda i, n: (i, 0)),
            pl.BlockSpec((k, tn), lambda i, n: (0, n)),
            pl.BlockSpec((tm, d), lambda i, n: (i, 0)),
            vec, vec,
        ],
        out_specs=pl.BlockSpec((tm, d), lambda i, n: (i, 0)),
        out_shape=jax.ShapeDtypeStruct((m, d), F32),
        scratch_shapes=[_stats_scratch(tm)],
        compiler_params=_params(("parallel", "arbitrary")),
        name="proj_ln",
    )(a, w, x, ln_g, ln_b)


def _ple_kernel(x_ref, wg_ref, p_ref, wp_ref, g_ref, b_ref, o_ref, xb_ref, stats_ref, *, nn, tn):
    n = pl.program_id(1)

    @pl.when(n == 0)
    def _():
        xb_ref[...] = x_ref[...].astype(BF16)

    gate = jax.nn.sigmoid(jnp.dot(xb_ref[...], wg_ref[...], preferred_element_type=F32))
    proj = jnp.dot(p_ref[...].astype(BF16), wp_ref[...], preferred_element_type=F32)
    o_ref[:, pl.ds(pl.multiple_of(n * tn, tn), tn)] = gate * proj

    @pl.when(n == nn - 1)
    def _():
        _residual_ln(x_ref, o_ref, g_ref, b_ref, o_ref, stats_ref, 1.0)


def _ple(x, wg, p, wp, ln_g, ln_b, *, tm=512, tn=512):
    m, d = x.shape
    pdim = p.shape[-1]
    nn = d // tn
    assert m % tm == 0 and d % tn == 0
    vec = pl.BlockSpec((1, d), lambda i, n: (0, 0))
    return pl.pallas_call(
        functools.partial(_ple_kernel, nn=nn, tn=tn),
        grid=(m // tm, nn),
        in_specs=[
            pl.BlockSpec((tm, d), lambda i, n: (i, 0)),
            pl.BlockSpec((d, tn), lambda i, n: (0, n)),
            pl.BlockSpec((tm, pdim), lambda i, n: (i, 0)),
            pl.BlockSpec((pdim, tn), lambda i, n: (0, n)),
            vec, vec,
        ],
        out_specs=pl.BlockSpec((tm, d), lambda i, n: (i, 0)),
        out_shape=jax.ShapeDtypeStruct((m, d), F32),
        scratch_shapes=[pltpu.VMEM((tm, d), BF16), _stats_scratch(tm)],
        compiler_params=_params(("parallel", "arbitrary")),
        name="ple",
    )(x, wg, p, wp, ln_g, ln_b)


def _gelu_exact(x):
    return 0.5 * x * (1.0 + lax.erf(x * math.sqrt(0.5)))


def _sgu_in_kernel(x_ref, w_ref, lg_ref, lb_ref, ws_ref, bt_ref, *rest, nn, tn, period, emit_v):
    if emit_v:
        v_ref, gated_ref, xb_ref, u_ref, stats_ref = rest
    else:
        gated_ref, xb_ref, u_ref, stats_ref, v_ref = rest
    n = pl.program_id(1)
    half = nn // 2

    @pl.when(n == 0)
    def _():
        xb_ref[...] = x_ref[...].astype(BF16)

    z = _gelu_exact(jnp.dot(xb_ref[...], w_ref[...], preferred_element_type=F32))

    @pl.when(n < half)
    def _():
        u_ref[:, pl.ds(pl.multiple_of(n * tn, tn), tn)] = z

    @pl.when(n >= half)
    def _():
        v_ref[:, pl.ds(pl.multiple_of((n - half) * tn, tn), tn)] = z

    @pl.when(n == nn - 1)
    def _():
        def store_v(rows, cols, y):
            v_ref[rows, cols] = y

        _layer_norm_sweeps(v_ref.shape[0], v_ref.shape[1], lambda rows, cols: v_ref[rows, cols],
                           lg_ref, lb_ref, store_v, stats_ref)

        t = lax.broadcasted_iota(jnp.int32, (CHUNK, CHUNK), 0)
        s = lax.broadcasted_iota(jnp.int32, (CHUNK, CHUNK), 1)
        keep = (s <= t) & ((t // period) == (s // period))
        gdim = v_ref.shape[1] // SGU_GROUPS
        for c in range(v_ref.shape[0] // CHUNK):
            rows = slice(c * CHUNK, (c + 1) * CHUNK)
            for g in range(SGU_GROUPS):
                cols = slice(g * gdim, (g + 1) * gdim)
                wm = jnp.where(keep, ws_ref[g], 0.0).astype(BF16)
                sg = jnp.dot(wm, v_ref[rows, cols].astype(BF16), preferred_element_type=F32)
                sg = sg + bt_ref[:, g:g + 1]
                gated_ref[rows, cols] = (u_ref[rows, cols] * sg).astype(BF16)


def _sgu_in(x, w_in, lg, lb, ws, bt, period, emit_v, *, tm=512, tn=256):
    m, d = x.shape
    inner = w_in.shape[-1] // 2
    nn = w_in.shape[-1] // tn
    assert m % tm == 0 and tm % CHUNK == 0 and inner % tn == 0
    row_block = pl.BlockSpec((tm, inner), lambda i, n: (i, 0))
    vec = pl.BlockSpec((1, inner), lambda i, n: (0, 0))
    out_specs = [row_block]
    out_shape = [jax.ShapeDtypeStruct((m, inner), BF16)]
    scratch = [pltpu.VMEM((tm, d), BF16), pltpu.VMEM((tm, inner), F32), _stats_scratch(tm)]
    if emit_v:
        out_specs.insert(0, row_block)
        out_shape.insert(0, jax.ShapeDtypeStruct((m, inner), F32))
    else:
        scratch.append(pltpu.VMEM((tm, inner), F32))
    outs = pl.pallas_call(
        functools.partial(_sgu_in_kernel, nn=nn, tn=tn, period=period, emit_v=emit_v),
        grid=(m // tm, nn),
        in_specs=[
            pl.BlockSpec((tm, d), lambda i, n: (i, 0)),
            pl.BlockSpec((d, tn), lambda i, n: (0, n)),
            vec, vec,
            pl.BlockSpec((SGU_GROUPS, CHUNK, CHUNK), lambda i, n: (0, 0, 0)),
            pl.BlockSpec((CHUNK, SGU_GROUPS), lambda i, n: (0, 0)),
        ],
        out_specs=out_specs,
        out_shape=out_shape,
        scratch_shapes=scratch,
        compiler_params=_params(("parallel", "arbitrary")),
        name="sgu_in",
    )(x, w_in, lg, lb, ws, bt)
    return tuple(outs) if emit_v else (None, outs[0])


def _glu_kernel(x_ref, wa_ref, wg_ref, o_ref, xb_ref):
    @pl.when(pl.program_id(1) == 0)
    def _():
        xb_ref[...] = x_ref[...].astype(BF16)

    xb = xb_ref[...]
    a = jnp.dot(xb, wa_ref[...], preferred_element_type=F32)
    g = jnp.dot(xb, wg_ref[...], preferred_element_type=F32)
    o_ref[...] = a * jax.nn.sigmoid(g)


def _glu(x, w_pw1, *, tm=512, tn=512):
    m, d = x.shape
    inner = w_pw1.shape[-1] // 2
    nn = inner // tn
    assert m % tm == 0 and inner % tn == 0
    return pl.pallas_call(
        _glu_kernel,
        grid=(m // tm, nn),
        in_specs=[
            pl.BlockSpec((tm, d), lambda i, n: (i, 0)),
            pl.BlockSpec((d, tn), lambda i, n: (0, n)),
            pl.BlockSpec((d, tn), lambda i, n: (0, n + nn)),
        ],
        out_specs=pl.BlockSpec((tm, tn), lambda i, n: (i, n)),
        out_shape=jax.ShapeDtypeStruct((m, inner), F32),
        scratch_shapes=[pltpu.VMEM((tm, d), BF16)],
        compiler_params=_params(("parallel", "arbitrary")),
        name="glu",
    )(x, w_pw1, w_pw1)


def _dwconv_block(blk, j0, n_out, w_ref, cols):
    acc = None
    for r in range(SUBLANES):
        taps = [k for k in range(CONV_WIDTH) if (j0 + k) % SUBLANES == r]
        if not taps:
            continue
        span = max(j0 + k - r for k in taps) + n_out
        shifted = blk[r:r + span]
        for k in taps:
            a = j0 + k - r
            term = w_ref[k:k + 1, cols] * shifted[a:a + n_out]
            acc = term if acc is None else acc + term
    return acc


def _conv_post(y_ref, bd_ref, g_ref, b_ref, o_ref, stats_ref):
    def store(rows, cols, y):
        o_ref[rows, cols] = y.astype(o_ref.dtype)

    _layer_norm_sweeps(y_ref.shape[0], y_ref.shape[1],
                       lambda rows, cols: y_ref[rows, cols] + bd_ref[:, cols],
                       g_ref, b_ref, store, stats_ref, slice_rows=BF16_ROWS,
                       post=lambda z: z * jax.nn.sigmoid(z))


def _conv_prompt_kernel(cur_ref, prev_ref, w_ref, bd_ref, g_ref, b_ref, o_ref,
                        win_ref, y_ref, stats_ref, *, t_rows):
    t = pl.program_id(1)

    @pl.when(t == 0)
    def _():
        win_ref[0:HALO, :] = jnp.zeros((HALO, win_ref.shape[1]), F32)

    @pl.when(t > 0)
    def _():
        win_ref[0:HALO, :] = prev_ref[...]

    win_ref[HALO:HALO + t_rows, :] = cur_ref[...]

    d = y_ref.shape[-1]
    base = HALO - (CONV_WIDTH - 1)
    n_out = HALO

    def body(rb, carry):
        r0 = pl.multiple_of(rb * n_out, n_out)
        for c0 in range(0, d, CONV_LANES):
            cols = slice(c0, c0 + CONV_LANES)
            blk = win_ref[pl.ds(r0, HALO + n_out), cols]
            y_ref[pl.ds(r0, n_out), cols] = _dwconv_block(blk, base, n_out, w_ref, cols)
        return carry

    lax.fori_loop(0, t_rows // n_out, body, 0)
    _conv_post(y_ref, bd_ref, g_ref, b_ref, o_ref, stats_ref)


def _conv_prompt(glu, w_dw, b_dw, ln_g, ln_b, *, t_rows=256):
    bsz, seq, d = glu.shape
    per = t_rows // HALO
    assert seq % t_rows == 0 and t_rows % HALO == 0
    vec = pl.BlockSpec((1, d), lambda b, t: (0, 0))
    return pl.pallas_call(
        functools.partial(_conv_prompt_kernel, t_rows=t_rows),
        grid=(bsz, seq // t_rows),
        in_specs=[
            pl.BlockSpec((None, t_rows, d), lambda b, t: (b, t, 0)),
            pl.BlockSpec((None, HALO, d), lambda b, t: (b, jnp.maximum(t * per - 1, 0), 0)),
            pl.BlockSpec((CONV_WIDTH, d), lambda b, t: (0, 0)),
            vec, vec, vec,
        ],
        out_specs=pl.BlockSpec((None, t_rows, d), lambda b, t: (b, t, 0)),
        out_shape=jax.ShapeDtypeStruct((bsz, seq, d), BF16),
        scratch_shapes=[pltpu.VMEM((HALO + t_rows, d), F32), pltpu.VMEM((t_rows, d), F32),
                        _stats_scratch(t_rows)],
        compiler_params=_params(("parallel", "arbitrary")),
        name="conv_prompt",
    )(glu, glu, w_dw, b_dw, ln_g, ln_b)


def _conv_sample_kernel(xpad_ref, w_ref, bd_ref, g_ref, b_ref, o_ref, y_ref, stats_ref, *, bb, seq):
    d = y_ref.shape[-1]
    for j in range(bb):
        for c0 in range(0, d, CONV_LANES):
            cols = slice(c0, c0 + CONV_LANES)
            y_ref[j * seq:(j + 1) * seq, cols] = _dwconv_block(xpad_ref[j, :, cols], 0, seq, w_ref, cols)
    _conv_post(y_ref, bd_ref, g_ref, b_ref, o_ref, stats_ref)


def _conv_sample(xpad, w_dw, b_dw, ln_g, ln_b, *, bb=2):
    bsz, padded, d = xpad.shape
    seq = padded - (CONV_WIDTH - 1)
    assert bsz % bb == 0 and seq % SUBLANES == 0 and (bb * seq) % BF16_ROWS == 0
    vec = pl.BlockSpec((1, d), lambda b: (0, 0))
    return pl.pallas_call(
        functools.partial(_conv_sample_kernel, bb=bb, seq=seq),
        grid=(bsz // bb,),
        in_specs=[
            pl.BlockSpec((bb, padded, d), lambda b: (b, 0, 0)),
            pl.BlockSpec((CONV_WIDTH, d), lambda b: (0, 0)),
            vec, vec, vec,
        ],
        out_specs=pl.BlockSpec((bb * seq, d), lambda b: (b, 0)),
        out_shape=jax.ShapeDtypeStruct((bsz * seq, d), BF16),
        scratch_shapes=[pltpu.VMEM((bb * seq, d), F32), _stats_scratch(bb * seq)],
        compiler_params=_params(("parallel",)),
        name="conv_sample",
    )(xpad, w_dw, b_dw, ln_g, ln_b)


def kernel(x_prompt, x_sample, state_conv, p_prompt, p_sample, ln_g, ln_b, ffn_w_gate, ffn_w_up, ffn_w_down, sgu_w_in, sgu_ln_g, sgu_ln_b, sgu_w_s, sgu_b_s, sgu_w_out, conv_w_pw1, conv_w_dw, conv_b_dw, conv_ln_g, conv_ln_b, conv_w_pw2, ple_w_gate, ple_w_proj):
    bsz, seq, d = x_prompt.shape
    dbsz, dseq, _ = x_sample.shape
    assert CHUNK % dseq == 0 and seq % CHUNK == 0

    wg = ffn_w_gate.astype(BF16)
    wu = ffn_w_up.astype(BF16)
    wd = ffn_w_down.astype(BF16)
    w_in = sgu_w_in.astype(BF16)
    w_out = sgu_w_out.astype(BF16)
    w_pw1 = conv_w_pw1.astype(BF16)
    w_pw2 = conv_w_pw2.astype(BF16)
    w_pg = ple_w_gate.astype(BF16)
    w_pp = ple_w_proj.astype(BF16)

    def row(v):
        return v.reshape(1, -1)

    y_p = x_prompt.reshape(bsz * seq, d)
    y_s = x_sample.reshape(dbsz * dseq, d)
    new_v_s, new_conv_p, new_conv_s = [], [], []

    for i in range(DEPTH):
        def ffn(y, which, slot):
            return _ffn(y, wg, wu, wd, i, which, row(ln_g[i, slot]), row(ln_b[i, slot]))

        y_p = ffn(y_p, 0, 0)
        y_s = ffn(y_s, 0, 0)

        j = i // N_MIXERS
        g1, b1 = row(ln_g[i, 1]), row(ln_b[i, 1])
        if i % N_MIXERS == 0:
            lg, lb = row(sgu_ln_g[j]), row(sgu_ln_b[j])
            reps = CHUNK // dseq
            ws_p = sgu_w_s[j]
            ws_s = jnp.tile(sgu_w_s[j][:, :dseq, :dseq], (1, reps, reps))
            bt_p = sgu_b_s[j].T
            bt_s = jnp.tile(sgu_b_s[j][:, :dseq], (1, reps)).T
            _, gated_p = _sgu_in(y_p, w_in[j], lg, lb, ws_p, bt_p, CHUNK, False)
            v_s, gated_s = _sgu_in(y_s, w_in[j], lg, lb, ws_s, bt_s, dseq, True, tm=256)
            new_v_s.append(v_s.reshape(dbsz, dseq, -1))
            y_p = _proj_ln(gated_p, w_out[j], y_p, g1, b1)
            y_s = _proj_ln(gated_s, w_out[j], y_s, g1, b1)
        else:
            cg, cb = row(conv_ln_g[j]), row(conv_ln_b[j])
            bd = row(conv_b_dw[j])
            glu_p = _glu(y_p, w_pw1[j]).reshape(bsz, seq, d)
            glu_s = _glu(y_s, w_pw1[j]).reshape(dbsz, dseq, d)
            xpad_s = jnp.concatenate([state_conv[j], glu_s], axis=1)
            c_p = _conv_prompt(glu_p, conv_w_dw[j], bd, cg, cb).reshape(bsz * seq, d)
            c_s = _conv_sample(xpad_s, conv_w_dw[j], bd, cg, cb)
            new_conv_p.append(glu_p[:, seq - (CONV_WIDTH - 1):])
            new_conv_s.append(xpad_s[:, dseq:])
            y_p = _proj_ln(c_p, w_pw2[j], y_p, g1, b1)
            y_s = _proj_ln(c_s, w_pw2[j], y_s, g1, b1)

        y_p = ffn(y_p, 1, 2)
        y_s = ffn(y_s, 1, 2)

        g3, b3 = row(ln_g[i, 3]), row(ln_b[i, 3])
        y_p = _ple(y_p, w_pg[i], p_prompt[i].reshape(bsz * seq, -1), w_pp[i], g3, b3)
        y_s = _ple(y_s, w_pg[i], p_sample[i].reshape(dbsz * dseq, -1), w_pp[i], g3, b3)

    return (y_p.reshape(bsz, seq, d), y_s.reshape(dbsz, dseq, d), jnp.stack(new_v_s),
            jnp.stack(new_conv_p), jnp.stack(new_conv_s))
```

```python
import functools
import math

import jax
import jax.numpy as jnp
from jax import lax
from jax.experimental import pallas as pl
from jax.experimental.pallas import tpu as pltpu

DEPTH = 2
N_MIXERS = 2
CHUNK = 128
SGU_GROUPS = 16
CONV_WIDTH = 31
ALPHA = (2 * DEPTH) ** 0.25
LN_EPS = 1e-5

SUBLANES = 8
LANES = 128
BF16_ROWS = 16
HALO = 32
CONV_ROWS = 64
FFN_TF = 256
VMEM_LIMIT_BYTES = 56 * 1024 * 1024

BF16 = jnp.bfloat16
F32 = jnp.float32


def _params(semantics):
    return pltpu.CompilerParams(dimension_semantics=semantics,
                                vmem_limit_bytes=VMEM_LIMIT_BYTES)


def _rowwise(n_rows, fn, group, slice_rows=SUBLANES):
    step = slice_rows * group
    assert n_rows % step == 0

    def body(c, carry):
        for s in range(group):
            fn(pl.ds(pl.multiple_of(c * step + s * slice_rows, slice_rows), slice_rows))
        return carry

    lax.fori_loop(0, n_rows // step, body, 0)


def _stats_scratch(n_rows):
    return pltpu.VMEM((2, n_rows, LANES), F32)


def _layer_norm_sweeps(n_rows, d, z_block, g_ref, b_ref, store_block, stats_ref,
                       slice_rows=SUBLANES, post=None):
    n_col = d // LANES
    inv_d = 1.0 / d

    def stats(rows):
        s1 = s2 = None
        for c in range(n_col):
            z = z_block(rows, slice(c * LANES, (c + 1) * LANES))
            s1 = z if s1 is None else s1 + z
            s2 = z * z if s2 is None else s2 + z * z
        mean = jnp.sum(s1, axis=-1, keepdims=True) * inv_d
        var = jnp.maximum(jnp.sum(s2, axis=-1, keepdims=True) * inv_d - mean * mean, 0.0)
        stats_ref[0, rows, :] = jnp.broadcast_to(mean, (slice_rows, LANES))
        stats_ref[1, rows, :] = jnp.broadcast_to(lax.rsqrt(var + LN_EPS), (slice_rows, LANES))

    def norm(rows):
        mean = stats_ref[0, rows, :]
        rstd = stats_ref[1, rows, :]
        for c in range(n_col):
            cols = slice(c * LANES, (c + 1) * LANES)
            y = (z_block(rows, cols) - mean) * rstd * g_ref[:, cols] + b_ref[:, cols]
            store_block(rows, cols, y if post is None else post(y))

    n_slices = n_rows // slice_rows
    _rowwise(n_rows, stats, math.gcd(n_slices, 8), slice_rows)
    _rowwise(n_rows, norm, math.gcd(n_slices, 4), slice_rows)


def _residual_ln(x_ref, d_ref, g_ref, b_ref, o_ref, stats_ref, delta_scale):
    def z_block(rows, cols):
        dlt = d_ref[rows, cols]
        if delta_scale != 1.0:
            dlt = delta_scale * dlt
        return ALPHA * x_ref[rows, cols] + dlt

    def store(rows, cols, y):
        o_ref[rows, cols] = y

    _layer_norm_sweeps(o_ref.shape[0], o_ref.shape[1], z_block, g_ref, b_ref, store, stats_ref)


def _ffn_kernel(x_ref, wg_ref, wu_ref, wd_ref, g_ref, b_ref, *rest, nf, n_side):
    side_in, rest = rest[:n_side], rest[n_side:]
    o_ref, side_out = rest[0], rest[1:1 + n_side]
    xb_ref, stats_ref = rest[1 + n_side:]
    for src, dst in zip(side_in, side_out):
        dst[...] = src[...].astype(BF16)
    f = pl.program_id(1)

    @pl.when(f == 0)
    def _():
        xb_ref[...] = x_ref[...].astype(BF16)
        o_ref[...] = jnp.zeros_like(o_ref)

    xb = xb_ref[...]
    gate = jnp.dot(xb, wg_ref[...], preferred_element_type=F32)
    up = jnp.dot(xb, wu_ref[...], preferred_element_type=F32)
    h = (gate * jax.nn.sigmoid(gate) * up).astype(BF16)
    o_ref[...] += jnp.dot(h, wd_ref[...], preferred_element_type=F32)

    @pl.when(f == nf - 1)
    def _():
        _residual_ln(x_ref, o_ref, g_ref, b_ref, o_ref, stats_ref, 0.5)


def _ffn(x, wg, wu, wd, ln_g, ln_b, cast_next=None, *, tm=512, tf=FFN_TF):
    m, d = x.shape
    dff = wg.shape[-1]
    nf, ni = dff // tf, m // tm
    assert m % tm == 0 and dff % tf == 0
    vec = pl.BlockSpec((1, d), lambda i, f: (0, 0))
    in_specs = [
        pl.BlockSpec((tm, d), lambda i, f: (i, 0)),
        pl.BlockSpec((d, tf), lambda i, f: (0, f)),
        pl.BlockSpec((d, tf), lambda i, f: (0, f)),
        pl.BlockSpec((tf, d), lambda i, f: (f, 0)),
        vec, vec,
    ]
    out_specs = [pl.BlockSpec((tm, d), lambda i, f: (i, 0))]
    out_shape = [jax.ShapeDtypeStruct((m, d), F32)]
    args = [x, wg, wu, wd, ln_g, ln_b]
    n_side = 0
    if cast_next is not None:
        gate, up, down, lead = cast_next
        slab = d // ni
        assert d % ni == 0 and slab % BF16_ROWS == 0 and slab % LANES == 0
        squeezed = (None,) * len(lead)
        in_specs += [pl.BlockSpec(squeezed + (slab, tf), lambda i, f: lead + (i, f))] * 2
        in_specs += [pl.BlockSpec(squeezed + (tf, slab), lambda i, f: lead + (f, i))]
        out_specs += [pl.BlockSpec((slab, tf), lambda i, f: (i, f))] * 2
        out_specs += [pl.BlockSpec((tf, slab), lambda i, f: (f, i))]
        out_shape += [jax.ShapeDtypeStruct((d, dff), BF16)] * 2
        out_shape += [jax.ShapeDtypeStruct((dff, d), BF16)]
        args += [gate, up, down]
        n_side = 3
    outs = pl.pallas_call(
        functools.partial(_ffn_kernel, nf=nf, n_side=n_side),
        grid=(ni, nf),
        in_specs=in_specs,
        out_specs=out_specs,
        out_shape=out_shape,
        scratch_shapes=[pltpu.VMEM((tm, d), BF16), _stats_scratch(tm)],
        compiler_params=_params(("parallel", "arbitrary")),
        name="ffn",
    )(*args)
    return outs if n_side else outs[0]


def _proj_ln_kernel(a_ref, w_ref, x_ref, g_ref, b_ref, o_ref, stats_ref, *, nn, tn):
    n = pl.program_id(1)
    acc = jnp.dot(a_ref[...], w_ref[...], preferred_element_type=F32)
    o_ref[:, pl.ds(pl.multiple_of(n * tn, tn), tn)] = acc

    @pl.when(n == nn - 1)
    def _():
        _residual_ln(x_ref, o_ref, g_ref, b_ref, o_ref, stats_ref, 1.0)


def _proj_ln(a, w, x, ln_g, ln_b, *, tm=512, tn=512):
    m, k = a.shape
    d = w.shape[-1]
    nn = d // tn
    assert m % tm == 0 and d % tn == 0
    vec = pl.BlockSpec((1, d), lambda i, n: (0, 0))
    return pl.pallas_call(
        functools.partial(_proj_ln_kernel, nn=nn, tn=tn),
        grid=(m // tm, nn),
        in_specs=[
            pl.BlockSpec((tm, k), lambda i, n: (i, 0)),
            pl.BlockSpec((k, tn), lambda i, n: (0, n)),
            pl.BlockSpec((tm, d), lambda i, n: (i, 0)),
            vec, vec,
        ],
        out_specs=pl.BlockSpec((tm, d), lambda i, n: (i, 0)),
        out_shape=jax.ShapeDtypeStruct((m, d), F32),
        scratch_shapes=[_stats_scratch(tm)],
        compiler_params=_params(("parallel", "arbitrary")),
        name="proj_ln",
    )(a, w, x, ln_g, ln_b)


def _ple_kernel(x_ref, wg_ref, p_ref, wp_ref, g_ref, b_ref, o_ref, xb_ref, stats_ref, *, nn, tn):
    n = pl.program_id(1)

    @pl.when(n == 0)
    def _():
        xb_ref[...] = x_ref[...].astype(BF16)

    gate = jax.nn.sigmoid(jnp.dot(xb_ref[...], wg_ref[...], preferred_element_type=F32))
    proj = jnp.dot(p_ref[...].astype(BF16), wp_ref[...], preferred_element_type=F32)
    o_ref[:, pl.ds(pl.multiple_of(n * tn, tn), tn)] = gate * proj

    @pl.when(n == nn - 1)
    def _():
        _residual_ln(x_ref, o_ref, g_ref, b_ref, o_ref, stats_ref, 1.0)


def _ple(x, wg, p, wp, ln_g, ln_b, *, tm=512, tn=512):
    m, d = x.shape
    pdim = p.shape[-1]
    nn = d // tn
    assert m % tm == 0 and d % tn == 0
    vec = pl.BlockSpec((1, d), lambda i, n: (0, 0))
    return pl.pallas_call(
        functools.partial(_ple_kernel, nn=nn, tn=tn),
        grid=(m // tm, nn),
        in_specs=[
            pl.BlockSpec((tm, d), lambda i, n: (i, 0)),
            pl.BlockSpec((d, tn), lambda i, n: (0, n)),
            pl.BlockSpec((tm, pdim), lambda i, n: (i, 0)),
            pl.BlockSpec((pdim, tn), lambda i, n: (0, n)),
            vec, vec,
        ],
        out_specs=pl.BlockSpec((tm, d), lambda i, n: (i, 0)),
        out_shape=jax.ShapeDtypeStruct((m, d), F32),
        scratch_shapes=[pltpu.VMEM((tm, d), BF16), _stats_scratch(tm)],
        compiler_params=_params(("parallel", "arbitrary")),
        name="ple",
    )(x, wg, p, wp, ln_g, ln_b)


def _gelu_exact(x):
    return 0.5 * x * (1.0 + lax.erf(x * math.sqrt(0.5)))


def _sgu_in_kernel(x_ref, w_ref, lg_ref, lb_ref, ws_ref, bt_ref, *rest, nn, tn, period, emit_v):
    if emit_v:
        v_ref, gated_ref, xb_ref, u_ref, stats_ref = rest
    else:
        gated_ref, xb_ref, u_ref, stats_ref, v_ref = rest
    n = pl.program_id(1)
    half = nn // 2

    @pl.when(n == 0)
    def _():
        xb_ref[...] = x_ref[...].astype(BF16)

    z = _gelu_exact(jnp.dot(xb_ref[...], w_ref[...], preferred_element_type=F32))

    @pl.when(n < half)
    def _():
        u_ref[:, pl.ds(pl.multiple_of(n * tn, tn), tn)] = z

    @pl.when(n >= half)
    def _():
        v_ref[:, pl.ds(pl.multiple_of((n - half) * tn, tn), tn)] = z

    @pl.when(n == nn - 1)
    def _():
        def store_v(rows, cols, y):
            v_ref[rows, cols] = y

        _layer_norm_sweeps(v_ref.shape[0], v_ref.shape[1], lambda rows, cols: v_ref[rows, cols],
                           lg_ref, lb_ref, store_v, stats_ref)

        t = lax.broadcasted_iota(jnp.int32, (CHUNK, CHUNK), 0)
        s = lax.broadcasted_iota(jnp.int32, (CHUNK, CHUNK), 1)
        keep = (s <= t) & ((t // period) == (s // period))
        gdim = v_ref.shape[1] // SGU_GROUPS
        for c in range(v_ref.shape[0] // CHUNK):
            rows = slice(c * CHUNK, (c + 1) * CHUNK)
            for g in range(SGU_GROUPS):
                cols = slice(g * gdim, (g + 1) * gdim)
                wm = jnp.where(keep, ws_ref[g], 0.0).astype(BF16)
                sg = jnp.dot(wm, v_ref[rows, cols].astype(BF16), preferred_element_type=F32)
                sg = sg + bt_ref[:, g:g + 1]
                gated_ref[rows, cols] = (u_ref[rows, cols] * sg).astype(BF16)


def _sgu_in(x, w_in, lg, lb, ws, bt, period, emit_v, *, tm=512, tn=256):
    m, d = x.shape
    inner = w_in.shape[-1] // 2
    nn = w_in.shape[-1] // tn
    assert m % tm == 0 and tm % CHUNK == 0 and inner % tn == 0
    row_block = pl.BlockSpec((tm, inner), lambda i, n: (i, 0))
    vec = pl.BlockSpec((1, inner), lambda i, n: (0, 0))
    out_specs = [row_block]
    out_shape = [jax.ShapeDtypeStruct((m, inner), BF16)]
    scratch = [pltpu.VMEM((tm, d), BF16), pltpu.VMEM((tm, inner), F32), _stats_scratch(tm)]
    if emit_v:
        out_specs.insert(0, row_block)
        out_shape.insert(0, jax.ShapeDtypeStruct((m, inner), F32))
    else:
        scratch.append(pltpu.VMEM((tm, inner), F32))
    outs = pl.pallas_call(
        functools.partial(_sgu_in_kernel, nn=nn, tn=tn, period=period, emit_v=emit_v),
        grid=(m // tm, nn),
        in_specs=[
            pl.BlockSpec((tm, d), lambda i, n: (i, 0)),
            pl.BlockSpec((d, tn), lambda i, n: (0, n)),
            vec, vec,
            pl.BlockSpec((SGU_GROUPS, CHUNK, CHUNK), lambda i, n: (0, 0, 0)),
            pl.BlockSpec((CHUNK, SGU_GROUPS), lambda i, n: (0, 0)),
        ],
        out_specs=out_specs,
        out_shape=out_shape,
        scratch_shapes=scratch,
        compiler_params=_params(("parallel", "arbitrary")),
        name="sgu_in",
    )(x, w_in, lg, lb, ws, bt)
    return tuple(outs) if emit_v else (None, outs[0])


def _glu_kernel(x_ref, wa_ref, wg_ref, o_ref, xb_ref):
    @pl.when(pl.program_id(1) == 0)
    def _():
        xb_ref[...] = x_ref[...].astype(BF16)

    xb = xb_ref[...]
    a = jnp.dot(xb, wa_ref[...], preferred_element_type=F32)
    g = jnp.dot(xb, wg_ref[...], preferred_element_type=F32)
    o_ref[...] = a * jax.nn.sigmoid(g)


def _glu(x, w_pw1, *, tm=512, tn=512):
    m, d = x.shape
    inner = w_pw1.shape[-1] // 2
    nn = inner // tn
    assert m % tm == 0 and inner % tn == 0
    return pl.pallas_call(
        _glu_kernel,
        grid=(m // tm, nn),
        in_specs=[
            pl.BlockSpec((tm, d), lambda i, n: (i, 0)),
            pl.BlockSpec((d, tn), lambda i, n: (0, n)),
            pl.BlockSpec((d, tn), lambda i, n: (0, n + nn)),
        ],
        out_specs=pl.BlockSpec((tm, tn), lambda i, n: (i, n)),
        out_shape=jax.ShapeDtypeStruct((m, inner), F32),
        scratch_shapes=[pltpu.VMEM((tm, d), BF16)],
        compiler_params=_params(("parallel", "arbitrary")),
        name="glu",
    )(x, w_pw1, w_pw1)


def _dwconv_column(load_rows, n_in, j0, n_out, wb_ref, cols):
    blk = [load_rows(j) for j in range(n_in)]
    sub = lax.broadcasted_iota(jnp.int32, (SUBLANES, LANES), 0)
    acc = [None] * n_out
    for r in range(SUBLANES):
        taps = [k for k in range(CONV_WIDTH) if (j0 + k) % SUBLANES == r]
        if not taps:
            continue
        if r == 0:
            shifted = blk
        else:
            rot = [pltpu.roll(b, SUBLANES - r, 0) for b in blk]
            keep = sub < (SUBLANES - r)
            shifted = [jnp.where(keep, rot[j], rot[j + 1]) for j in range(n_in - 1)]
        for k in taps:
            a = (j0 + k - r) // SUBLANES
            wk = wb_ref[k, :, cols]
            for i in range(n_out):
                term = wk * shifted[a + i]
                acc[i] = term if acc[i] is None else acc[i] + term
    return acc


def _conv_post(y_ref, bd_ref, g_ref, b_ref, o_ref, stats_ref):
    def store(rows, cols, y):
        o_ref[rows, cols] = y.astype(o_ref.dtype)

    _layer_norm_sweeps(y_ref.shape[0], y_ref.shape[1],
                       lambda rows, cols: y_ref[rows, cols] + bd_ref[:, cols],
                       g_ref, b_ref, store, stats_ref, slice_rows=BF16_ROWS,
                       post=lambda z: z * jax.nn.sigmoid(z))


def _conv_prompt_kernel(cur_ref, prev_ref, wb_ref, bd_ref, g_ref, b_ref, o_ref,
                        win_ref, y_ref, stats_ref, *, t_rows):
    t = pl.program_id(1)

    @pl.when(t == 0)
    def _():
        win_ref[0:HALO, :] = jnp.zeros((HALO, win_ref.shape[1]), F32)

    @pl.when(t > 0)
    def _():
        win_ref[0:HALO, :] = prev_ref[...]

    win_ref[HALO:HALO + t_rows, :] = cur_ref[...]

    d = y_ref.shape[-1]
    base = HALO - (CONV_WIDTH - 1)
    n_out = CONV_ROWS // SUBLANES
    n_in = n_out + HALO // SUBLANES

    def body(rb, carry):
        r0 = pl.multiple_of(rb * CONV_ROWS, CONV_ROWS)
        for c0 in range(0, d, LANES):
            cols = slice(c0, c0 + LANES)
            ys = _dwconv_column(lambda j: win_ref[pl.ds(r0 + j * SUBLANES, SUBLANES), cols],
                                n_in, base, n_out, wb_ref, cols)
            for i, y in enumerate(ys):
                y_ref[pl.ds(r0 + i * SUBLANES, SUBLANES), cols] = y
        return carry

    lax.fori_loop(0, t_rows // CONV_ROWS, body, 0)
    _conv_post(y_ref, bd_ref, g_ref, b_ref, o_ref, stats_ref)


def _conv_prompt(glu, wb, b_dw, ln_g, ln_b, *, t_rows=256):
    bsz, seq, d = glu.shape
    per = t_rows // HALO
    assert seq % t_rows == 0 and t_rows % HALO == 0 and t_rows % CONV_ROWS == 0
    vec = pl.BlockSpec((1, d), lambda b, t: (0, 0))
    return pl.pallas_call(
        functools.partial(_conv_prompt_kernel, t_rows=t_rows),
        grid=(bsz, seq // t_rows),
        in_specs=[
            pl.BlockSpec((None, t_rows, d), lambda b, t: (b, t, 0)),
            pl.BlockSpec((None, HALO, d), lambda b, t: (b, jnp.maximum(t * per - 1, 0), 0)),
            pl.BlockSpec((CONV_WIDTH, SUBLANES, d), lambda b, t: (0, 0, 0)),
            vec, vec, vec,
        ],
        out_specs=pl.BlockSpec((None, t_rows, d), lambda b, t: (b, t, 0)),
        out_shape=jax.ShapeDtypeStruct((bsz, seq, d), BF16),
        scratch_shapes=[pltpu.VMEM((HALO + t_rows, d), F32), pltpu.VMEM((t_rows, d), F32),
                        _stats_scratch(t_rows)],
        compiler_params=_params(("parallel", "arbitrary")),
        name="conv_prompt",
    )(glu, glu, wb, b_dw, ln_g, ln_b)


def _conv_sample_kernel(xpad_ref, wb_ref, bd_ref, g_ref, b_ref, o_ref, y_ref, stats_ref, *, bb, seq):
    d = y_ref.shape[-1]
    n_out = seq // SUBLANES
    n_in = xpad_ref.shape[1] // SUBLANES

    def body(j, carry):
        for c0 in range(0, d, LANES):
            cols = slice(c0, c0 + LANES)
            ys = _dwconv_column(lambda q: xpad_ref[j, q * SUBLANES:(q + 1) * SUBLANES, cols],
                                n_in, 0, n_out, wb_ref, cols)
            for i, y in enumerate(ys):
                y_ref[pl.ds(pl.multiple_of(j * seq + i * SUBLANES, SUBLANES), SUBLANES), cols] = y
        return carry

    lax.fori_loop(0, bb, body, 0)
    _conv_post(y_ref, bd_ref, g_ref, b_ref, o_ref, stats_ref)


def _conv_sample(xpad, wb, b_dw, ln_g, ln_b, seq, *, bb=8):
    bsz, padded, d = xpad.shape
    assert bsz % bb == 0 and seq % SUBLANES == 0 and (bb * seq) % BF16_ROWS == 0
    assert padded % SUBLANES == 0 and padded >= seq + CONV_WIDTH - 1
    vec = pl.BlockSpec((1, d), lambda b: (0, 0))
    return pl.pallas_call(
        functools.partial(_conv_sample_kernel, bb=bb, seq=seq),
        grid=(bsz // bb,),
        in_specs=[
            pl.BlockSpec((bb, padded, d), lambda b: (b, 0, 0)),
            pl.BlockSpec((CONV_WIDTH, SUBLANES, d), lambda b: (0, 0, 0)),
            vec, vec, vec,
        ],
        out_specs=pl.BlockSpec((bb * seq, d), lambda b: (b, 0)),
        out_shape=jax.ShapeDtypeStruct((bsz * seq, d), BF16),
        scratch_shapes=[pltpu.VMEM((bb * seq, d), F32), _stats_scratch(bb * seq)],
        compiler_params=_params(("parallel",)),
        name="conv_sample",
    )(xpad, wb, b_dw, ln_g, ln_b)


def kernel(x_prompt, x_sample, state_conv, p_prompt, p_sample, ln_g, ln_b, ffn_w_gate, ffn_w_up, ffn_w_down, sgu_w_in, sgu_ln_g, sgu_ln_b, sgu_w_s, sgu_b_s, sgu_w_out, conv_w_pw1, conv_w_dw, conv_b_dw, conv_ln_g, conv_ln_b, conv_w_pw2, ple_w_gate, ple_w_proj):
    bsz, seq, d = x_prompt.shape
    dbsz, dseq, _ = x_sample.shape
    assert CHUNK % dseq == 0 and seq % CHUNK == 0

    ffn_f32 = (ffn_w_gate, ffn_w_up, ffn_w_down)
    ffn_order = [(layer, which) for layer in range(DEPTH) for which in range(2)]
    ffn_bf16 = {ffn_order[0]: tuple(w[ffn_order[0]].astype(BF16) for w in ffn_f32)}
    w_in = sgu_w_in.astype(BF16)
    w_out = sgu_w_out.astype(BF16)
    w_pw1 = conv_w_pw1.astype(BF16)
    w_pw2 = conv_w_pw2.astype(BF16)
    w_pg = ple_w_gate.astype(BF16)
    w_pp = ple_w_proj.astype(BF16)

    def row(v):
        return v.reshape(1, -1)

    y_p = x_prompt.reshape(bsz * seq, d)
    y_s = x_sample.reshape(dbsz * dseq, d)
    new_v_s, new_conv_p, new_conv_s = [], [], []

    def ffn_pair(y_p, y_s, layer, which, slot):
        weights = ffn_bf16[(layer, which)]
        g, b = row(ln_g[layer, slot]), row(ln_b[layer, slot])
        pos = ffn_order.index((layer, which))
        if pos + 1 < len(ffn_order):
            nxt = ffn_order[pos + 1]
            y_p, *ffn_bf16[nxt] = _ffn(y_p, *weights, g, b, cast_next=ffn_f32 + (nxt,))
        else:
            y_p = _ffn(y_p, *weights, g, b)
        return y_p, _ffn(y_s, *weights, g, b)

    for i in range(DEPTH):
        y_p, y_s = ffn_pair(y_p, y_s, i, 0, 0)

        j = i // N_MIXERS
        g1, b1 = row(ln_g[i, 1]), row(ln_b[i, 1])
        if i % N_MIXERS == 0:
            lg, lb = row(sgu_ln_g[j]), row(sgu_ln_b[j])
            reps = CHUNK // dseq
            ws_p = sgu_w_s[j]
            ws_s = jnp.tile(sgu_w_s[j][:, :dseq, :dseq], (1, reps, reps))
            bt_p = sgu_b_s[j].T
            bt_s = jnp.tile(sgu_b_s[j][:, :dseq], (1, reps)).T
            _, gated_p = _sgu_in(y_p, w_in[j], lg, lb, ws_p, bt_p, CHUNK, False)
            v_s, gated_s = _sgu_in(y_s, w_in[j], lg, lb, ws_s, bt_s, dseq, True, tm=256)
            new_v_s.append(v_s.reshape(dbsz, dseq, -1))
            y_p = _proj_ln(gated_p, w_out[j], y_p, g1, b1)
            y_s = _proj_ln(gated_s, w_out[j], y_s, g1, b1)
        else:
            cg, cb = row(conv_ln_g[j]), row(conv_ln_b[j])
            bd = row(conv_b_dw[j])
            glu_p = _glu(y_p, w_pw1[j]).reshape(bsz, seq, d)
            glu_s = _glu(y_s, w_pw1[j]).reshape(dbsz, dseq, d)
            look_back = CONV_WIDTH - 1
            fill = -(look_back + dseq) % SUBLANES
            xpad_s = jnp.concatenate(
                [state_conv[j], glu_s, jnp.zeros((dbsz, fill, d), F32)], axis=1)
            wb = jnp.broadcast_to(conv_w_dw[j][:, None, :], (CONV_WIDTH, SUBLANES, d))
            c_p = _conv_prompt(glu_p, wb, bd, cg, cb).reshape(bsz * seq, d)
            c_s = _conv_sample(xpad_s, wb, bd, cg, cb, dseq)
            new_conv_p.append(glu_p[:, seq - look_back:])
            new_conv_s.append(xpad_s[:, dseq:dseq + look_back])
            y_p = _proj_ln(c_p, w_pw2[j], y_p, g1, b1)
            y_s = _proj_ln(c_s, w_pw2[j], y_s, g1, b1)

        y_p, y_s = ffn_pair(y_p, y_s, i, 1, 2)

        g3, b3 = row(ln_g[i, 3]), row(ln_b[i, 3])
        y_p = _ple(y_p, w_pg[i], p_prompt[i].reshape(bsz * seq, -1), w_pp[i], g3, b3)
        y_s = _ple(y_s, w_pg[i], p_sample[i].reshape(dbsz * dseq, -1), w_pp[i], g3, b3)

    return (y_p.reshape(bsz, seq, d), y_s.reshape(dbsz, dseq, d), jnp.stack(new_v_s),
            jnp.stack(new_conv_p), jnp.stack(new_conv_s))
```

```python
import functools
import math

import jax
import jax.numpy as jnp
from jax import lax
from jax.experimental import pallas as pl
from jax.experimental.pallas import tpu as pltpu

DEPTH = 2
N_MIXERS = 2
CHUNK = 128
SGU_GROUPS = 16
CONV_WIDTH = 31
ALPHA = (2 * DEPTH) ** 0.25
LN_EPS = 1e-5

SUBLANES = 8
LANES = 128
BF16_ROWS = 16
HALO = 32
CONV_ROWS = 64
FFN_TF = 256
FFN_IO_CHUNKS = 8
VMEM_LIMIT_BYTES = 56 * 1024 * 1024

BF16 = jnp.bfloat16
F32 = jnp.float32


def _params(semantics):
    return pltpu.CompilerParams(dimension_semantics=semantics,
                                vmem_limit_bytes=VMEM_LIMIT_BYTES)


def _rowwise(n_rows, fn, group, slice_rows=SUBLANES):
    step = slice_rows * group
    assert n_rows % step == 0

    def body(c, carry):
        for s in range(group):
            fn(pl.ds(pl.multiple_of(c * step + s * slice_rows, slice_rows), slice_rows))
        return carry

    lax.fori_loop(0, n_rows // step, body, 0)


def _stats_scratch(n_rows):
    return pltpu.VMEM((2, n_rows, LANES), F32)


def _layer_norm_sweeps(n_rows, d, z_block, g_ref, b_ref, store_block, stats_ref,
                       slice_rows=SUBLANES, post=None):
    n_col = d // LANES
    inv_d = 1.0 / d

    def stats(rows):
        s1 = s2 = None
        for c in range(n_col):
            z = z_block(rows, slice(c * LANES, (c + 1) * LANES))
            s1 = z if s1 is None else s1 + z
            s2 = z * z if s2 is None else s2 + z * z
        mean = jnp.sum(s1, axis=-1, keepdims=True) * inv_d
        var = jnp.maximum(jnp.sum(s2, axis=-1, keepdims=True) * inv_d - mean * mean, 0.0)
        stats_ref[0, rows, :] = jnp.broadcast_to(mean, (slice_rows, LANES))
        stats_ref[1, rows, :] = jnp.broadcast_to(lax.rsqrt(var + LN_EPS), (slice_rows, LANES))

    def norm(rows):
        mean = stats_ref[0, rows, :]
        rstd = stats_ref[1, rows, :]
        for c in range(n_col):
            cols = slice(c * LANES, (c + 1) * LANES)
            y = (z_block(rows, cols) - mean) * rstd * g_ref[:, cols] + b_ref[:, cols]
            store_block(rows, cols, y if post is None else post(y))

    n_slices = n_rows // slice_rows
    _rowwise(n_rows, stats, math.gcd(n_slices, 8), slice_rows)
    _rowwise(n_rows, norm, math.gcd(n_slices, 4), slice_rows)


def _residual_ln(x_ref, d_ref, g_ref, b_ref, o_ref, stats_ref, delta_scale):
    def z_block(rows, cols):
        dlt = d_ref[rows, cols]
        if delta_scale != 1.0:
            dlt = delta_scale * dlt
        return ALPHA * x_ref[rows, cols] + dlt

    def store(rows, cols, y):
        o_ref[rows, cols] = y

    _layer_norm_sweeps(o_ref.shape[0], o_ref.shape[1], z_block, g_ref, b_ref, store, stats_ref)


def _ffn_kernel(x_hbm, wg_ref, wu_ref, wd_ref, g_ref, b_ref, *rest, nf, ni, n_side):
    side_in, rest = rest[:n_side], rest[n_side:]
    y_hbm, side_out = rest[0], rest[1:1 + n_side]
    acc_ref, xb_ref, stats_ref, in_sem, out_sem = rest[1 + n_side:]
    for src, dst in zip(side_in, side_out):
        dst[...] = src[...].astype(BF16)
    i = pl.program_id(0)
    f = pl.program_id(1)
    tm, d = acc_ref.shape
    rc = tm // FFN_IO_CHUNKS

    def chunk_rows(c):
        return pl.ds(c * rc, rc)

    def x_copy(tile, c):
        return pltpu.make_async_copy(x_hbm.at[pl.ds(tile * tm + c * rc, rc)],
                                     acc_ref.at[chunk_rows(c)], in_sem.at[c])

    def y_copy(tile, c):
        return pltpu.make_async_copy(acc_ref.at[chunk_rows(c)],
                                     y_hbm.at[pl.ds(tile * tm + c * rc, rc)], out_sem.at[c])

    @pl.when(f == 0)
    def _():
        @pl.when(i == 0)
        def _():
            for c in range(FFN_IO_CHUNKS):
                x_copy(0, c).start()

        for c in range(FFN_IO_CHUNKS):
            x_copy(i, c).wait()
            x = acc_ref[chunk_rows(c), :]
            xb_ref[chunk_rows(c), :] = x.astype(BF16)
            acc_ref[chunk_rows(c), :] = (ALPHA / 0.5) * x

    xb = xb_ref[...]
    gate = jnp.dot(xb, wg_ref[...], preferred_element_type=F32)
    up = jnp.dot(xb, wu_ref[...], preferred_element_type=F32)
    h = (gate * jax.nn.sigmoid(gate) * up).astype(BF16)
    acc_ref[...] += jnp.dot(h, wd_ref[...], preferred_element_type=F32)

    @pl.when(f == nf - 1)
    def _():
        lag = 2
        for c in range(FFN_IO_CHUNKS + lag):
            if c < FFN_IO_CHUNKS:
                chunk = acc_ref.at[chunk_rows(c)]

                def store(rows, cols, y, chunk=chunk):
                    chunk[rows, cols] = y

                _layer_norm_sweeps(rc, d, lambda rows, cols, chunk=chunk: 0.5 * chunk[rows, cols],
                                   g_ref, b_ref, store, stats_ref)
                y_copy(i, c).start()
            if c >= lag:
                y_copy(i, c - lag).wait()

                @pl.when(i + 1 < ni)
                def _(c=c):
                    x_copy(i + 1, c - lag).start()


def _ffn(x, wg, wu, wd, ln_g, ln_b, cast_next=None, *, tm=1024, tf=FFN_TF):
    m, d = x.shape
    dff = wg.shape[-1]
    nf, ni = dff // tf, m // tm
    assert m % tm == 0 and dff % tf == 0
    vec = pl.BlockSpec((1, d), lambda i, f: (0, 0))
    rc = tm // FFN_IO_CHUNKS
    assert tm % FFN_IO_CHUNKS == 0 and rc % (SUBLANES * 8) == 0
    in_specs = [
        pl.BlockSpec(memory_space=pl.ANY),
        pl.BlockSpec((d, tf), lambda i, f: (0, f)),
        pl.BlockSpec((d, tf), lambda i, f: (0, f)),
        pl.BlockSpec((tf, d), lambda i, f: (f, 0)),
        vec, vec,
    ]
    out_specs = [pl.BlockSpec(memory_space=pl.ANY)]
    out_shape = [jax.ShapeDtypeStruct((m, d), F32)]
    args = [x, wg, wu, wd, ln_g, ln_b]
    n_side = 0
    if cast_next is not None:
        gate, up, down, lead = cast_next
        slab = d // ni
        assert d % ni == 0 and slab % BF16_ROWS == 0 and slab % LANES == 0
        squeezed = (None,) * len(lead)
        in_specs += [pl.BlockSpec(squeezed + (slab, tf), lambda i, f: lead + (i, f))] * 2
        in_specs += [pl.BlockSpec(squeezed + (tf, slab), lambda i, f: lead + (f, i))]
        out_specs += [pl.BlockSpec((slab, tf), lambda i, f: (i, f))] * 2
        out_specs += [pl.BlockSpec((tf, slab), lambda i, f: (f, i))]
        out_shape += [jax.ShapeDtypeStruct((d, dff), BF16)] * 2
        out_shape += [jax.ShapeDtypeStruct((dff, d), BF16)]
        args += [gate, up, down]
        n_side = 3
    outs = pl.pallas_call(
        functools.partial(_ffn_kernel, nf=nf, ni=ni, n_side=n_side),
        grid=(ni, nf),
        in_specs=in_specs,
        out_specs=out_specs,
        out_shape=out_shape,
        scratch_shapes=[pltpu.VMEM((tm, d), F32), pltpu.VMEM((tm, d), BF16), _stats_scratch(rc),
                        pltpu.SemaphoreType.DMA((FFN_IO_CHUNKS,)),
                        pltpu.SemaphoreType.DMA((FFN_IO_CHUNKS,))],
        compiler_params=_params(("arbitrary", "arbitrary")),
        name="ffn",
    )(*args)
    return outs if n_side else outs[0]


def _proj_ln_kernel(a_ref, w_ref, x_ref, g_ref, b_ref, o_ref, stats_ref, *, nn, tn):
    n = pl.program_id(1)
    acc = jnp.dot(a_ref[...], w_ref[...], preferred_element_type=F32)
    o_ref[:, pl.ds(pl.multiple_of(n * tn, tn), tn)] = acc

    @pl.when(n == nn - 1)
    def _():
        _residual_ln(x_ref, o_ref, g_ref, b_ref, o_ref, stats_ref, 1.0)


def _proj_ln(a, w, x, ln_g, ln_b, *, tm=512, tn=512):
    m, k = a.shape
    d = w.shape[-1]
    nn = d // tn
    assert m % tm == 0 and d % tn == 0
    vec = pl.BlockSpec((1, d), lambda i, n: (0, 0))
    return pl.pallas_call(
        functools.partial(_proj_ln_kernel, nn=nn, tn=tn),
        grid=(m // tm, nn),
        in_specs=[
            pl.BlockSpec((tm, k), lambda i, n: (i, 0)),
            pl.BlockSpec((k, tn), lambda i, n: (0, n)),
            pl.BlockSpec((tm, d), lambda i, n: (i, 0)),
            vec, vec,
        ],
        out_specs=pl.BlockSpec((tm, d), lambda i, n: (i, 0)),
        out_shape=jax.ShapeDtypeStruct((m, d), F32),
        scratch_shapes=[_stats_scratch(tm)],
        compiler_params=_params(("parallel", "arbitrary")),
        name="proj_ln",
    )(a, w, x, ln_g, ln_b)


def _ple_kernel(x_ref, wg_ref, p_ref, wp_ref, g_ref, b_ref, o_ref, xb_ref, stats_ref, *, nn, tn):
    n = pl.program_id(1)

    @pl.when(n == 0)
    def _():
        xb_ref[...] = x_ref[...].astype(BF16)

    gate = jax.nn.sigmoid(jnp.dot(xb_ref[...], wg_ref[...], preferred_element_type=F32))
    proj = jnp.dot(p_ref[...].astype(BF16), wp_ref[...], preferred_element_type=F32)
    o_ref[:, pl.ds(pl.multiple_of(n * tn, tn), tn)] = gate * proj

    @pl.when(n == nn - 1)
    def _():
        _residual_ln(x_ref, o_ref, g_ref, b_ref, o_ref, stats_ref, 1.0)


def _ple(x, wg, p, wp, ln_g, ln_b, *, tm=512, tn=512):
    m, d = x.shape
    pdim = p.shape[-1]
    nn = d // tn
    assert m % tm == 0 and d % tn == 0
    vec = pl.BlockSpec((1, d), lambda i, n: (0, 0))
    return pl.pallas_call(
        functools.partial(_ple_kernel, nn=nn, tn=tn),
        grid=(m // tm, nn),
        in_specs=[
            pl.BlockSpec((tm, d), lambda i, n: (i, 0)),
            pl.BlockSpec((d, tn), lambda i, n: (0, n)),
            pl.BlockSpec((tm, pdim), lambda i, n: (i, 0)),
            pl.BlockSpec((pdim, tn), lambda i, n: (0, n)),
            vec, vec,
        ],
        out_specs=pl.BlockSpec((tm, d), lambda i, n: (i, 0)),
        out_shape=jax.ShapeDtypeStruct((m, d), F32),
        scratch_shapes=[pltpu.VMEM((tm, d), BF16), _stats_scratch(tm)],
        compiler_params=_params(("parallel", "arbitrary")),
        name="ple",
    )(x, wg, p, wp, ln_g, ln_b)


def _gelu_exact(x):
    return 0.5 * x * (1.0 + lax.erf(x * math.sqrt(0.5)))


def _sgu_in_kernel(x_ref, w_ref, lg_ref, lb_ref, ws_ref, bt_ref, *rest, nn, tn, period, emit_v):
    if emit_v:
        v_ref, gated_ref, xb_ref, u_ref, stats_ref = rest
    else:
        gated_ref, xb_ref, u_ref, stats_ref, v_ref = rest
    n = pl.program_id(1)
    half = nn // 2

    @pl.when(n == 0)
    def _():
        xb_ref[...] = x_ref[...].astype(BF16)

    z = _gelu_exact(jnp.dot(xb_ref[...], w_ref[...], preferred_element_type=F32))

    @pl.when(n < half)
    def _():
        u_ref[:, pl.ds(pl.multiple_of(n * tn, tn), tn)] = z.astype(BF16)

    @pl.when(n >= half)
    def _():
        v_ref[:, pl.ds(pl.multiple_of((n - half) * tn, tn), tn)] = z

    @pl.when(n == nn - 1)
    def _():
        def store_v(rows, cols, y):
            v_ref[rows, cols] = y

        _layer_norm_sweeps(v_ref.shape[0], v_ref.shape[1], lambda rows, cols: v_ref[rows, cols],
                           lg_ref, lb_ref, store_v, stats_ref)

        t = lax.broadcasted_iota(jnp.int32, (CHUNK, CHUNK), 0)
        s = lax.broadcasted_iota(jnp.int32, (CHUNK, CHUNK), 1)
        keep = (s <= t) & ((t // period) == (s // period))
        gdim = v_ref.shape[1] // SGU_GROUPS
        for c in range(v_ref.shape[0] // CHUNK):
            rows = slice(c * CHUNK, (c + 1) * CHUNK)
            for g in range(SGU_GROUPS):
                cols = slice(g * gdim, (g + 1) * gdim)
                wm = jnp.where(keep, ws_ref[g], 0.0).astype(BF16)
                sg = jnp.dot(wm, v_ref[rows, cols].astype(BF16), preferred_element_type=F32)
                sg = sg + bt_ref[:, g:g + 1]
                gated_ref[rows, cols] = (u_ref[rows, cols] * sg).astype(BF16)


def _sgu_in(x, w_in, lg, lb, ws, bt, period, emit_v, *, tm=512, tn=256):
    m, d = x.shape
    inner = w_in.shape[-1] // 2
    nn = w_in.shape[-1] // tn
    assert m % tm == 0 and tm % CHUNK == 0 and inner % tn == 0
    row_block = pl.BlockSpec((tm, inner), lambda i, n: (i, 0))
    vec = pl.BlockSpec((1, inner), lambda i, n: (0, 0))
    out_specs = [row_block]
    out_shape = [jax.ShapeDtypeStruct((m, inner), BF16)]
    scratch = [pltpu.VMEM((tm, d), BF16), pltpu.VMEM((tm, inner), BF16), _stats_scratch(tm)]
    if emit_v:
        out_specs.insert(0, row_block)
        out_shape.insert(0, jax.ShapeDtypeStruct((m, inner), F32))
    else:
        scratch.append(pltpu.VMEM((tm, inner), F32))
    outs = pl.pallas_call(
        functools.partial(_sgu_in_kernel, nn=nn, tn=tn, period=period, emit_v=emit_v),
        grid=(m // tm, nn),
        in_specs=[
            pl.BlockSpec((tm, d), lambda i, n: (i, 0)),
            pl.BlockSpec((d, tn), lambda i, n: (0, n)),
            vec, vec,
            pl.BlockSpec((SGU_GROUPS, CHUNK, CHUNK), lambda i, n: (0, 0, 0)),
            pl.BlockSpec((CHUNK, SGU_GROUPS), lambda i, n: (0, 0)),
        ],
        out_specs=out_specs,
        out_shape=out_shape,
        scratch_shapes=scratch,
        compiler_params=_params(("parallel", "arbitrary")),
        name="sgu_in",
    )(x, w_in, lg, lb, ws, bt)
    return tuple(outs) if emit_v else (None, outs[0])


def _glu_kernel(x_ref, wa_ref, wg_ref, o_ref, xb_ref):
    @pl.when(pl.program_id(1) == 0)
    def _():
        xb_ref[...] = x_ref[...].astype(BF16)

    xb = xb_ref[...]
    a = jnp.dot(xb, wa_ref[...], preferred_element_type=F32)
    g = jnp.dot(xb, wg_ref[...], preferred_element_type=F32)
    o_ref[...] = a * jax.nn.sigmoid(g)


def _glu(x, w_pw1, *, tm=512, tn=512):
    m, d = x.shape
    inner = w_pw1.shape[-1] // 2
    nn = inner // tn
    assert m % tm == 0 and inner % tn == 0
    return pl.pallas_call(
        _glu_kernel,
        grid=(m // tm, nn),
        in_specs=[
            pl.BlockSpec((tm, d), lambda i, n: (i, 0)),
            pl.BlockSpec((d, tn), lambda i, n: (0, n)),
            pl.BlockSpec((d, tn), lambda i, n: (0, n + nn)),
        ],
        out_specs=pl.BlockSpec((tm, tn), lambda i, n: (i, n)),
        out_shape=jax.ShapeDtypeStruct((m, inner), F32),
        scratch_shapes=[pltpu.VMEM((tm, d), BF16)],
        compiler_params=_params(("parallel", "arbitrary")),
        name="glu",
    )(x, w_pw1, w_pw1)


def _dwconv_column(load_rows, n_in, j0, n_out, wb_ref, cols):
    blk = [load_rows(j) for j in range(n_in)]
    sub = lax.broadcasted_iota(jnp.int32, (SUBLANES, LANES), 0)
    acc = [None] * n_out
    for r in range(SUBLANES):
        taps = [k for k in range(CONV_WIDTH) if (j0 + k) % SUBLANES == r]
        if not taps:
            continue
        if r == 0:
            shifted = blk
        else:
            rot = [pltpu.roll(b, SUBLANES - r, 0) for b in blk]
            keep = sub < (SUBLANES - r)
            shifted = [jnp.where(keep, rot[j], rot[j + 1]) for j in range(n_in - 1)]
        for k in taps:
            a = (j0 + k - r) // SUBLANES
            wk = wb_ref[k, :, cols]
            for i in range(n_out):
                term = wk * shifted[a + i]
                acc[i] = term if acc[i] is None else acc[i] + term
    return acc


def _conv_post(y_ref, bd_ref, g_ref, b_ref, o_ref, stats_ref):
    def store(rows, cols, y):
        o_ref[rows, cols] = y.astype(o_ref.dtype)

    _layer_norm_sweeps(y_ref.shape[0], y_ref.shape[1],
                       lambda rows, cols: y_ref[rows, cols] + bd_ref[:, cols],
                       g_ref, b_ref, store, stats_ref, slice_rows=BF16_ROWS,
                       post=lambda z: z * jax.nn.sigmoid(z))


def _conv_prompt_kernel(cur_ref, prev_ref, wb_ref, bd_ref, g_ref, b_ref, o_ref,
                        win_ref, y_ref, stats_ref, *, t_rows):
    t = pl.program_id(1)

    @pl.when(t == 0)
    def _():
        win_ref[0:HALO, :] = jnp.zeros((HALO, win_ref.shape[1]), F32)

    @pl.when(t > 0)
    def _():
        win_ref[0:HALO, :] = prev_ref[...]

    win_ref[HALO:HALO + t_rows, :] = cur_ref[...]

    d = y_ref.shape[-1]
    base = HALO - (CONV_WIDTH - 1)
    n_out = CONV_ROWS // SUBLANES
    n_in = n_out + HALO // SUBLANES

    def body(rb, carry):
        r0 = pl.multiple_of(rb * CONV_ROWS, CONV_ROWS)
        for c0 in range(0, d, LANES):
            cols = slice(c0, c0 + LANES)
            ys = _dwconv_column(lambda j: win_ref[pl.ds(r0 + j * SUBLANES, SUBLANES), cols],
                                n_in, base, n_out, wb_ref, cols)
            for i, y in enumerate(ys):
                y_ref[pl.ds(r0 + i * SUBLANES, SUBLANES), cols] = y
        return carry

    lax.fori_loop(0, t_rows // CONV_ROWS, body, 0)
    _conv_post(y_ref, bd_ref, g_ref, b_ref, o_ref, stats_ref)


def _conv_prompt(glu, wb, b_dw, ln_g, ln_b, *, t_rows=256):
    bsz, seq, d = glu.shape
    per = t_rows // HALO
    assert seq % t_rows == 0 and t_rows % HALO == 0 and t_rows % CONV_ROWS == 0
    vec = pl.BlockSpec((1, d), lambda b, t: (0, 0))
    return pl.pallas_call(
        functools.partial(_conv_prompt_kernel, t_rows=t_rows),
        grid=(bsz, seq // t_rows),
        in_specs=[
            pl.BlockSpec((None, t_rows, d), lambda b, t: (b, t, 0)),
            pl.BlockSpec((None, HALO, d), lambda b, t: (b, jnp.maximum(t * per - 1, 0), 0)),
            pl.BlockSpec((CONV_WIDTH, SUBLANES, d), lambda b, t: (0, 0, 0)),
            vec, vec, vec,
        ],
        out_specs=pl.BlockSpec((None, t_rows, d), lambda b, t: (b, t, 0)),
        out_shape=jax.ShapeDtypeStruct((bsz, seq, d), BF16),
        scratch_shapes=[pltpu.VMEM((HALO + t_rows, d), F32), pltpu.VMEM((t_rows, d), F32),
                        _stats_scratch(t_rows)],
        compiler_params=_params(("parallel", "arbitrary")),
        name="conv_prompt",
    )(glu, glu, wb, b_dw, ln_g, ln_b)


def _conv_sample_kernel(xpad_ref, wb_ref, bd_ref, g_ref, b_ref, o_ref, y_ref, stats_ref, *, bb, seq):
    d = y_ref.shape[-1]
    n_out = seq // SUBLANES
    n_in = xpad_ref.shape[1] // SUBLANES

    def body(j, carry):
        for c0 in range(0, d, LANES):
            cols = slice(c0, c0 + LANES)
            ys = _dwconv_column(lambda q: xpad_ref[j, q * SUBLANES:(q + 1) * SUBLANES, cols],
                                n_in, 0, n_out, wb_ref, cols)
            for i, y in enumerate(ys):
                y_ref[pl.ds(pl.multiple_of(j * seq + i * SUBLANES, SUBLANES), SUBLANES), cols] = y
        return carry

    lax.fori_loop(0, bb, body, 0)
    _conv_post(y_ref, bd_ref, g_ref, b_ref, o_ref, stats_ref)


def _conv_sample(xpad, wb, b_dw, ln_g, ln_b, seq, *, bb=8):
    bsz, padded, d = xpad.shape
    assert bsz % bb == 0 and seq % SUBLANES == 0 and (bb * seq) % BF16_ROWS == 0
    assert padded % SUBLANES == 0 and padded >= seq + CONV_WIDTH - 1
    vec = pl.BlockSpec((1, d), lambda b: (0, 0))
    return pl.pallas_call(
        functools.partial(_conv_sample_kernel, bb=bb, seq=seq),
        grid=(bsz // bb,),
        in_specs=[
            pl.BlockSpec((bb, padded, d), lambda b: (b, 0, 0)),
            pl.BlockSpec((CONV_WIDTH, SUBLANES, d), lambda b: (0, 0, 0)),
            vec, vec, vec,
        ],
        out_specs=pl.BlockSpec((bb * seq, d), lambda b: (b, 0)),
        out_shape=jax.ShapeDtypeStruct((bsz * seq, d), BF16),
        scratch_shapes=[pltpu.VMEM((bb * seq, d), F32), _stats_scratch(bb * seq)],
        compiler_params=_params(("parallel",)),
        name="conv_sample",
    )(xpad, wb, b_dw, ln_g, ln_b)


def kernel(x_prompt, x_sample, state_conv, p_prompt, p_sample, ln_g, ln_b, ffn_w_gate, ffn_w_up, ffn_w_down, sgu_w_in, sgu_ln_g, sgu_ln_b, sgu_w_s, sgu_b_s, sgu_w_out, conv_w_pw1, conv_w_dw, conv_b_dw, conv_ln_g, conv_ln_b, conv_w_pw2, ple_w_gate, ple_w_proj):
    bsz, seq, d = x_prompt.shape
    dbsz, dseq, _ = x_sample.shape
    assert CHUNK % dseq == 0 and seq % CHUNK == 0

    ffn_f32 = (ffn_w_gate, ffn_w_up, ffn_w_down)
    ffn_order = [(layer, which) for layer in range(DEPTH) for which in range(2)]
    ffn_bf16 = {ffn_order[0]: tuple(w[ffn_order[0]].astype(BF16) for w in ffn_f32)}
    w_in = sgu_w_in.astype(BF16)
    w_out = sgu_w_out.astype(BF16)
    w_pw1 = conv_w_pw1.astype(BF16)
    w_pw2 = conv_w_pw2.astype(BF16)
    w_pg = ple_w_gate.astype(BF16)
    w_pp = ple_w_proj.astype(BF16)

    def row(v):
        return v.reshape(1, -1)

    y_p = x_prompt.reshape(bsz * seq, d)
    y_s = x_sample.reshape(dbsz * dseq, d)
    new_v_s, new_conv_p, new_conv_s = [], [], []

    def ffn_pair(y_p, y_s, layer, which, slot):
        weights = ffn_bf16[(layer, which)]
        g, b = row(ln_g[layer, slot]), row(ln_b[layer, slot])
        pos = ffn_order.index((layer, which))
        if pos + 1 < len(ffn_order):
            nxt = ffn_order[pos + 1]
            y_p, *ffn_bf16[nxt] = _ffn(y_p, *weights, g, b, cast_next=ffn_f32 + (nxt,))
        else:
            y_p = _ffn(y_p, *weights, g, b)
        return y_p, _ffn(y_s, *weights, g, b)

    for i in range(DEPTH):
        y_p, y_s = ffn_pair(y_p, y_s, i, 0, 0)

        j = i // N_MIXERS
        g1, b1 = row(ln_g[i, 1]), row(ln_b[i, 1])
        if i % N_MIXERS == 0:
            lg, lb = row(sgu_ln_g[j]), row(sgu_ln_b[j])
            reps = CHUNK // dseq
            ws_p = sgu_w_s[j]
            ws_s = jnp.tile(sgu_w_s[j][:, :dseq, :dseq], (1, reps, reps))
            bt_p = sgu_b_s[j].T
            bt_s = jnp.tile(sgu_b_s[j][:, :dseq], (1, reps)).T
            _, gated_p = _sgu_in(y_p, w_in[j], lg, lb, ws_p, bt_p, CHUNK, False, tn=512)
            v_s, gated_s = _sgu_in(y_s, w_in[j], lg, lb, ws_s, bt_s, dseq, True, tm=256)
            new_v_s.append(v_s.reshape(dbsz, dseq, -1))
            y_p = _proj_ln(gated_p, w_out[j], y_p, g1, b1)
            y_s = _proj_ln(gated_s, w_out[j], y_s, g1, b1)
        else:
            cg, cb = row(conv_ln_g[j]), row(conv_ln_b[j])
            bd = row(conv_b_dw[j])
            glu_p = _glu(y_p, w_pw1[j]).reshape(bsz, seq, d)
            glu_s = _glu(y_s, w_pw1[j]).reshape(dbsz, dseq, d)
            look_back = CONV_WIDTH - 1
            fill = -(look_back + dseq) % SUBLANES
            xpad_s = jnp.concatenate(
                [state_conv[j], glu_s, jnp.zeros((dbsz, fill, d), F32)], axis=1)
            wb = jnp.broadcast_to(conv_w_dw[j][:, None, :], (CONV_WIDTH, SUBLANES, d))
            c_p = _conv_prompt(glu_p, wb, bd, cg, cb).reshape(bsz * seq, d)
            c_s = _conv_sample(xpad_s, wb, bd, cg, cb, dseq)
            new_conv_p.append(glu_p[:, seq - look_back:])
            new_conv_s.append(xpad_s[:, dseq:dseq + look_back])
            y_p = _proj_ln(c_p, w_pw2[j], y_p, g1, b1)
            y_s = _proj_ln(c_s, w_pw2[j], y_s, g1, b1)

        y_p, y_s = ffn_pair(y_p, y_s, i, 1, 2)

        g3, b3 = row(ln_g[i, 3]), row(ln_b[i, 3])
        y_p = _ple(y_p, w_pg[i], p_prompt[i].reshape(bsz * seq, -1), w_pp[i], g3, b3)
        y_s = _ple(y_s, w_pg[i], p_sample[i].reshape(dbsz * dseq, -1), w_pp[i], g3, b3)

    return (y_p.reshape(bsz, seq, d), y_s.reshape(dbsz, dseq, d), jnp.stack(new_v_s),
            jnp.stack(new_conv_p), jnp.stack(new_conv_s))
```
